```python
import math
import jax, jax.numpy as jnp
from jax import lax
import numpy as np

D_MODEL = 2048
BATCH = 4
SEQ = 4096
DEPTH = 4

CHUNK = 64
N_MIXERS = 2
N_ATTN_LAYERS = (DEPTH + 1) // 2
N_GLA_LAYERS = DEPTH // 2

DIFF_HEAD_DIM = 128
DIFF_HEADS = D_MODEL // (2 * DIFF_HEAD_DIM)
Q_BLOCK = 128

GLA_HEADS = 4
GLA_KEY_DIM = D_MODEL // 2
GLA_VAL_DIM = D_MODEL
GLA_DK = GLA_KEY_DIM // GLA_HEADS
GLA_DV = GLA_VAL_DIM // GLA_HEADS
GLA_GATE_RANK = 16
GLA_GATE_TAU = 16.0

D_FF = 4 * D_MODEL

DN_ALPHA = (2.0 * DEPTH) ** 0.25
DN_BETA = (8.0 * DEPTH) ** -0.25
LN_EPS = 1e-5
RMS_EPS = 1e-6

kernel_name = "hybrid_diffattn_gla_deepnorm_encoder"


def layer_norm(x, g, b):
    xf = x.astype(jnp.float32)
    mu = jnp.mean(xf, axis=-1, keepdims=True)
    xc = xf - mu
    var = jnp.mean(xc * xc, axis=-1, keepdims=True)
    y = xc * lax.rsqrt(var + LN_EPS) * g.astype(jnp.float32) + b.astype(jnp.float32)
    return y.astype(x.dtype)


def rms_norm(x, g):
    xf = x.astype(jnp.float32)
    y = xf * lax.rsqrt(jnp.mean(xf * xf, axis=-1, keepdims=True) + RMS_EPS) * g.astype(jnp.float32)
    return y.astype(x.dtype)


def diff_attention(x, w_qkv, w_o, lam, subln, lam_init):
    B, S, _ = x.shape
    H, d = DIFF_HEADS, DIFF_HEAD_DIM
    qkv = x @ w_qkv
    q, k, v = jnp.split(qkv, [2 * H * d, 4 * H * d], axis=-1)
    q = q.reshape(B, S, H, 2, d) * (d ** -0.5)
    k = k.reshape(B, S, H, 2, d)
    v = v.reshape(B, S, H, 2 * d)
    lamf = lam.astype(jnp.float32)
    lam_full = (jnp.exp(jnp.sum(lamf[0] * lamf[1])) - jnp.exp(jnp.sum(lamf[2] * lamf[3]))
                + lam_init)
    k_chunk = jnp.arange(S) // CHUNK

    def block(i):
        q_blk = lax.dynamic_slice_in_dim(q, i * Q_BLOCK, Q_BLOCK, axis=1)
        s = jnp.einsum('bqhjd,bkhjd->bhjqk', q_blk, k).astype(jnp.float32)
        q_chunk = (i * Q_BLOCK + jnp.arange(Q_BLOCK)) // CHUNK
        allowed = k_chunk[None, :] <= q_chunk[:, None]
        s = jnp.where(allowed, s, -jnp.inf)
        p = jax.nn.softmax(s, axis=-1)
        w = p[:, :, 0] - lam_full * p[:, :, 1]
        return jnp.einsum('bhqk,bkhe->bqhe', w.astype(v.dtype), v)

    o = lax.map(block, jnp.arange(S // Q_BLOCK))
    o = o.transpose(1, 0, 2, 3, 4).reshape(B, S, H, 2 * d)
    o = rms_norm(o, subln) * (1.0 - lam_init)
    return o.reshape(B, S, H * 2 * d) @ w_o


def gla(x, w_in, w_gk2, b_gk, g_norm, w_o):
    B, S, _ = x.shape
    H, dk, dv = GLA_HEADS, GLA_DK, GLA_DV
    nc = S // CHUNK
    proj = x @ w_in
    q, k, v, r, z = jnp.split(
        proj, [GLA_KEY_DIM, 2 * GLA_KEY_DIM, 2 * GLA_KEY_DIM + GLA_VAL_DIM,
               2 * GLA_KEY_DIM + 2 * GLA_VAL_DIM], axis=-1)
    gk = z @ w_gk2 + b_gk
    log_a = jax.nn.log_sigmoid(gk.astype(jnp.float32)) / GLA_GATE_TAU

    def to_chunks(t, dh):
        return t.astype(jnp.float32).reshape(B, nc, CHUNK, H, dh).transpose(1, 0, 3, 2, 4)

    q = to_chunks(q, dk) * (dk ** -0.5)
    k = to_chunks(k, dk)
    v = to_chunks(v, dv)
    b = jnp.cumsum(to_chunks(log_a, dk), axis=3)
    b_last = b[:, :, :, -1:, :]
    q_dec = q * jnp.exp(b)
    k_inv = k * jnp.exp(-b)
    k_st = k * jnp.exp(b_last - b)

    causal = jnp.tril(jnp.ones((CHUNK, CHUNK), dtype=bool))
    attn = jnp.einsum('nbhtd,nbhsd->nbhts', q_dec, k_inv)
    attn = jnp.where(causal, attn, 0.0)
    o_intra = jnp.einsum('nbhts,nbhsv->nbhtv', attn, v)

    def step(state, inp):
        q_c, k_c, v_c, bl_c = inp
        o_c = jnp.einsum('bhtk,bhkv->bhtv', q_c, state)
        state = (jnp.exp(bl_c)[:, :, 0, :, None] * state
                 + jnp.einsum('bhsk,bhsv->bhkv', k_c, v_c))
        return state, o_c

    state0 = jnp.zeros((B, H, dk, dv), jnp.float32)
    _, o_inter = lax.scan(step, state0, (q_dec, k_st, v, b_last))
    o = (o_intra + o_inter).transpose(1, 0, 3, 2, 4).reshape(B, S, H, dv)
    o = rms_norm(o, g_norm).reshape(B, S, GLA_VAL_DIM) * jax.nn.silu(r.astype(jnp.float32))
    return o.astype(x.dtype) @ w_o


def squared_relu_mlp(x, w1, w2):
    return jnp.square(jax.nn.relu(x @ w1)) @ w2


def setup_inputs(seed: int = 0) -> dict:
    key = jax.random.key(seed)
    ks = jax.random.split(key, 20)
    D = D_MODEL
    NA, NG = N_ATTN_LAYERS, N_GLA_LAYERS

    def nrm(k, shape, scale):
        return jax.random.normal(k, shape, jnp.float32) * scale

    x = nrm(ks[0], (BATCH, SEQ, D), 1.0)

    qk_cols = 4 * DIFF_HEADS * DIFF_HEAD_DIM
    v_cols = 2 * DIFF_HEADS * DIFF_HEAD_DIM
    attn_w_qkv = jnp.concatenate(
        [nrm(ks[1], (NA, D, qk_cols), D ** -0.5),
         nrm(ks[2], (NA, D, v_cols), D ** -0.5 * DN_BETA)], axis=-1)
    attn_w_o = nrm(ks[3], (NA, v_cols, D), v_cols ** -0.5 * DN_BETA)
    attn_lambda = nrm(ks[4], (NA, 4, DIFF_HEAD_DIM), 0.1)
    attn_subln = 1.0 + nrm(ks[5], (NA, 2 * DIFF_HEAD_DIM), 0.01)

    gla_w_in = jnp.concatenate(
        [nrm(ks[6], (NG, D, 2 * GLA_KEY_DIM), D ** -0.5),
         nrm(ks[7], (NG, D, GLA_VAL_DIM), D ** -0.5 * DN_BETA),
         nrm(ks[8], (NG, D, GLA_VAL_DIM), D ** -0.5),
         nrm(ks[9], (NG, D, GLA_GATE_RANK), D ** -0.5)], axis=-1)
    gla_w_gk2 = nrm(ks[10], (NG, GLA_GATE_RANK, GLA_KEY_DIM), GLA_GATE_RANK ** -0.5)
    gla_b_gk = nrm(ks[11], (NG, GLA_KEY_DIM), 0.01)
    gla_norm = 1.0 + nrm(ks[12], (NG, GLA_DV), 0.01)
    gla_w_o = nrm(ks[13], (NG, GLA_VAL_DIM, D), GLA_VAL_DIM ** -0.5 * DN_BETA)

    mlp_w1 = nrm(ks[14], (DEPTH, D, D_FF), D ** -0.5 * DN_BETA)
    mlp_w2 = nrm(ks[15], (DEPTH, D_FF, D), D_FF ** -0.5 * DN_BETA)
    ln_g = 1.0 + nrm(ks[16], (DEPTH, 2, D), 0.01)
    ln_b = nrm(ks[17], (DEPTH, 2, D), 0.01)

    return {"x": x,
            "attn_w_qkv": attn_w_qkv, "attn_w_o": attn_w_o,
            "attn_lambda": attn_lambda, "attn_subln": attn_subln,
            "gla_w_in": gla_w_in, "gla_w_gk2": gla_w_gk2, "gla_b_gk": gla_b_gk,
            "gla_norm": gla_norm, "gla_w_o": gla_w_o,
            "mlp_w1": mlp_w1, "mlp_w2": mlp_w2,
            "ln_g": ln_g, "ln_b": ln_b}


def reference(x, attn_w_qkv, attn_w_o, attn_lambda, attn_subln,
              gla_w_in, gla_w_gk2, gla_b_gk, gla_norm, gla_w_o,
              mlp_w1, mlp_w2, ln_g, ln_b):
    h = x
    for i in range(DEPTH):
        j = i // N_MIXERS
        if i % N_MIXERS == 0:
            lam_init = 0.8 - 0.6 * math.exp(-0.3 * i)
            y = diff_attention(h, attn_w_qkv[j], attn_w_o[j], attn_lambda[j],
                               attn_subln[j], lam_init)
        else:
            y = gla(h, gla_w_in[j], gla_w_gk2[j], gla_b_gk[j], gla_norm[j], gla_w_o[j])
        h = layer_norm(DN_ALPHA * h + y, ln_g[i, 0], ln_b[i, 0])
        f = squared_relu_mlp(h, mlp_w1[i], mlp_w2[i])
        h = layer_norm(DN_ALPHA * h + f, ln_g[i, 1], ln_b[i, 1])
    return h
```

```python
import functools
import math

import jax
import jax.numpy as jnp
from jax import lax
from jax.experimental import pallas as pl
from jax.experimental.pallas import tpu as pltpu

D_MODEL = 2048
DEPTH = 4
CHUNK = 64

DIFF_HEAD_DIM = 128
DIFF_HEADS = D_MODEL // (2 * DIFF_HEAD_DIM)

GLA_HEADS = 4
GLA_KEY_DIM = D_MODEL // 2
GLA_VAL_DIM = D_MODEL
GLA_DK = GLA_KEY_DIM // GLA_HEADS
GLA_DV = GLA_VAL_DIM // GLA_HEADS
GLA_GATE_RANK = 16
GLA_GATE_TAU = 16.0

D_FF = 4 * D_MODEL

DN_ALPHA = (2.0 * DEPTH) ** 0.25
LN_EPS = 1e-5
RMS_EPS = 1e-6

LANES = 128
VMEM_LIMIT = 56 * 1024 * 1024

F32 = jnp.float32
BF16 = jnp.bfloat16

_NT = (((1,), (1,)), ((), ()))
_TN = (((0,), (0,)), ((), ()))


def _params(*sem):
    return pltpu.CompilerParams(dimension_semantics=sem, vmem_limit_bytes=VMEM_LIMIT)


def _layer_norm(xf, g, b):
    mu = jnp.mean(xf, axis=-1, keepdims=True)
    xc = xf - mu
    var = jnp.mean(xc * xc, axis=-1, keepdims=True)
    return xc * lax.rsqrt(var + LN_EPS) * g + b


def _proj_kernel(x_ref, w_ref, s_ref, o_ref):
    acc = jnp.dot(x_ref[...], w_ref[...], preferred_element_type=F32)
    o_ref[...] = (acc * s_ref[...]).astype(o_ref.dtype)


def _proj(x, w, colscale, *, tm, tn, name):
    m, k = x.shape
    n = w.shape[1]
    return pl.pallas_call(
        _proj_kernel,
        grid=(m // tm, n // tn),
        in_specs=[pl.BlockSpec((tm, k), lambda i, j: (i, 0)),
                  pl.BlockSpec((k, tn), lambda i, j: (0, j)),
                  pl.BlockSpec((1, tn), lambda i, j: (0, j))],
        out_specs=pl.BlockSpec((tm, tn), lambda i, j: (i, j)),
        out_shape=jax.ShapeDtypeStruct((m, n), BF16),
        compiler_params=_params("parallel", "parallel"),
        name=name,
    )(x, w, colscale)


def _out_ln_kernel(a_ref, w_ref, h_ref, g_ref, b_ref, of_ref, ob_ref):
    y = jnp.dot(a_ref[...], w_ref[...], preferred_element_type=F32)
    hn = _layer_norm(DN_ALPHA * h_ref[...] + y, g_ref[...], b_ref[...])
    of_ref[...] = hn
    ob_ref[...] = hn.astype(BF16)


def _out_ln(a, w, h, g, b, *, tm, name):
    m, k = a.shape
    d = w.shape[1]
    row = lambda i: (i, 0)
    const = lambda i: (0, 0)
    return pl.pallas_call(
        _out_ln_kernel,
        grid=(m // tm,),
        in_specs=[pl.BlockSpec((tm, k), row),
                  pl.BlockSpec((k, d), const),
                  pl.BlockSpec((tm, d), row),
                  pl.BlockSpec((1, d), const),
                  pl.BlockSpec((1, d), const)],
        out_specs=[pl.BlockSpec((tm, d), row), pl.BlockSpec((tm, d), row)],
        out_shape=[jax.ShapeDtypeStruct((m, d), F32),
                   jax.ShapeDtypeStruct((m, d), BF16)],
        compiler_params=_params("parallel"),
        name=name,
    )(a, w, h, g, b)


def _mlp_kernel(x_ref, w1_ref, w2_ref, h_ref, g_ref, b_ref, of_ref, ob_ref, acc_ref):
    f = pl.program_id(1)
    u = jnp.dot(x_ref[...], w1_ref[...], preferred_element_type=F32)
    u = jnp.maximum(u, 0.0)
    part = jnp.dot((u * u).astype(BF16), w2_ref[...], preferred_element_type=F32)

    @pl.when(f == 0)
    def _():
        acc_ref[...] = part

    @pl.when(f > 0)
    def _():
        acc_ref[...] += part

    @pl.when(f == pl.num_programs(1) - 1)
    def _():
        hn = _layer_norm(DN_ALPHA * h_ref[...] + acc_ref[...], g_ref[...], b_ref[...])
        of_ref[...] = hn
        ob_ref[...] = hn.astype(BF16)


def _mlp(x, w1, w2, h, g, b, *, tm, tf, name):
    m, d = x.shape
    ff = w1.shape[1]
    row = lambda i, f: (i, 0)
    const = lambda i, f: (0, 0)
    return pl.pallas_call(
        _mlp_kernel,
        grid=(m // tm, ff // tf),
        in_specs=[pl.BlockSpec((tm, d), row),
                  pl.BlockSpec((d, tf), lambda i, f: (0, f)),
                  pl.BlockSpec((tf, d), lambda i, f: (f, 0)),
                  pl.BlockSpec((tm, d), row),
                  pl.BlockSpec((1, d), const),
                  pl.BlockSpec((1, d), const)],
        out_specs=[pl.BlockSpec((tm, d), row), pl.BlockSpec((tm, d), row)],
        out_shape=[jax.ShapeDtypeStruct((m, d), F32),
                   jax.ShapeDtypeStruct((m, d), BF16)],
        scratch_shapes=[pltpu.VMEM((tm, d), F32)],
        compiler_params=_params("parallel", "arbitrary"),
        name=name,
    )(x, w1, w2, h, g, b)


def _attn_kernel(lam_ref, sub_ref, q_ref, k_ref, v_ref, o_ref, acc_ref, *, lam_init, t):
    d = DIFF_HEAD_DIM
    i = pl.program_id(2)

    lam = lam_ref[...]
    lam_full = (jnp.exp(jnp.sum(lam[0:1] * lam[1:2], keepdims=True))
                - jnp.exp(jnp.sum(lam[2:3] * lam[3:4], keepdims=True)) + lam_init)

    rows = lax.broadcasted_iota(jnp.int32, (t, t), 0) // CHUNK
    cols = lax.broadcasted_iota(jnp.int32, (t, t), 1) // CHUNK
    allowed = cols <= rows

    acc_ref[...] = jnp.zeros_like(acc_ref)

    def step(j, carry, masked):
        off = pl.multiple_of(j * t, t)
        vb = v_ref[pl.ds(off, t), :]
        out = []
        for a in range(2):
            m, l = carry[2 * a], carry[2 * a + 1]
            qa = q_ref[:, a * d:(a + 1) * d]
            ka = k_ref[pl.ds(off, t), a * d:(a + 1) * d]
            s = lax.dot_general(qa, ka, _NT, preferred_element_type=F32)
            if masked:
                s = jnp.where(allowed, s, -jnp.inf)
            mn = jnp.maximum(m, jnp.max(s, axis=-1, keepdims=True))
            p = jnp.exp(s - mn)
            c = jnp.exp(m - mn)
            ln = c * l + jnp.sum(p, axis=-1, keepdims=True)
            acc_ref[a] = c * acc_ref[a] + jnp.dot(p.astype(BF16), vb,
                                                   preferred_element_type=F32)
            out += [mn, ln]
        return tuple(out)

    neg = jnp.full((t, 1), -jnp.inf, F32)
    zero = jnp.zeros((t, 1), F32)
    carry = lax.fori_loop(0, i, functools.partial(step, masked=False),
                          (neg, zero, neg, zero))
    _, l0, _, l1 = step(i, carry, masked=True)

    o = acc_ref[0] / l0 - lam_full * (acc_ref[1] / l1)
    ms = jnp.mean(o * o, axis=-1, keepdims=True)
    o = o * lax.rsqrt(ms + RMS_EPS) * sub_ref[...] * (1.0 - lam_init)
    o_ref[...] = o.astype(o_ref.dtype)


def _diff_attention(qkv, lam, subln, *, batch, seq, lam_init, t, name):
    h2d = 2 * DIFF_HEAD_DIM
    nh = DIFF_HEADS
    nq = seq // t
    return pl.pallas_call(
        functools.partial(_attn_kernel, lam_init=lam_init, t=t),
        grid=(batch, nh, nq),
        in_specs=[pl.BlockSpec((4, DIFF_HEAD_DIM), lambda b, h, i: (0, 0)),
                  pl.BlockSpec((1, h2d), lambda b, h, i: (0, 0)),
                  pl.BlockSpec((t, h2d), lambda b, h, i: (b * nq + i, h)),
                  pl.BlockSpec((seq, h2d), lambda b, h, i: (b, nh + h)),
                  pl.BlockSpec((seq, h2d), lambda b, h, i: (b, 2 * nh + h))],
        out_specs=pl.BlockSpec((t, h2d), lambda b, h, i: (b * nq + i, h)),
        out_shape=jax.ShapeDtypeStruct((batch * seq, nh * h2d), BF16),
        scratch_shapes=[pltpu.VMEM((2, t, h2d), F32)],
        compiler_params=_params("parallel", "parallel", "arbitrary"),
        name=name,
    )(lam, subln, qkv, qkv, qkv)


def _gla_kernel(q_ref, k_ref, v_ref, r_ref, z_ref, wg_ref, bg_ref, gn_ref, o_ref,
                st_ref, oacc_ref, *, t):
    @pl.when(pl.program_id(2) == 0)
    def _():
        st_ref[...] = jnp.zeros_like(st_ref)

    gk = jnp.dot(z_ref[...], wg_ref[...], preferred_element_type=F32) + bg_ref[...]
    log_a = jax.nn.log_sigmoid(gk) / GLA_GATE_TAU

    pos = lax.broadcasted_iota(jnp.int32, (t, GLA_DK), 0) % CHUNK
    b = log_a
    shift = 1
    while shift < CHUNK:
        b = b + jnp.where(pos >= shift, pltpu.roll(b, shift, axis=0), 0.0)
        shift *= 2

    tr = lax.broadcasted_iota(jnp.int32, (CHUNK, CHUNK), 0)
    tc = lax.broadcasted_iota(jnp.int32, (CHUNK, CHUNK), 1)
    causal = tc <= tr

    for c in range(t // CHUNK):
        sl = slice(c * CHUNK, (c + 1) * CHUNK)
        bc = b[sl]
        bl = bc[CHUNK - 1:CHUNK]
        qc = q_ref[sl, :].astype(F32)
        kc = k_ref[sl, :].astype(F32)
        vc = v_ref[sl, :]
        q_dec = (qc * jnp.exp(bc)).astype(BF16)
        k_inv = (kc * jnp.exp(-bc)).astype(BF16)
        k_st = (kc * jnp.exp(bl - bc)).astype(BF16)
        attn = lax.dot_general(q_dec, k_inv, _NT, preferred_element_type=F32)
        attn = jnp.where(causal, attn, 0.0).astype(BF16)
        st = st_ref[...]
        oacc_ref[sl, :] = (
            jnp.dot(attn, vc, preferred_element_type=F32)
            + lax.dot_general(q_dec, st.astype(BF16), _NT, preferred_element_type=F32))
        st_ref[...] = jnp.exp(bl) * st + lax.dot_general(
            vc, k_st, _TN, preferred_element_type=F32)

    o = oacc_ref[...]
    ms = jnp.mean(o * o, axis=-1, keepdims=True)
    o = o * lax.rsqrt(ms + RMS_EPS) * gn_ref[...]
    r = r_ref[...].astype(F32)
    o = o * (r * jax.nn.sigmoid(r))
    o_ref[...] = o.astype(o_ref.dtype)


def _gla(proj, z, wg, bg, gn, *, batch, seq, t, name):
    nh = GLA_HEADS
    nt = seq // t
    tok = lambda off: (lambda b, h, i: (b * nt + i, off + h))
    return pl.pallas_call(
        functools.partial(_gla_kernel, t=t),
        grid=(batch, nh, nt),
        in_specs=[pl.BlockSpec((t, GLA_DK), tok(0)),
                  pl.BlockSpec((t, GLA_DK), tok(nh)),
                  pl.BlockSpec((t, GLA_DV), tok(nh)),
                  pl.BlockSpec((t, GLA_DV), tok(2 * nh)),
                  pl.BlockSpec((t, LANES), lambda b, h, i: (b * nt + i, 0)),
                  pl.BlockSpec((LANES, GLA_DK), lambda b, h, i: (0, h)),
                  pl.BlockSpec((1, GLA_DK), lambda b, h, i: (0, h)),
                  pl.BlockSpec((1, GLA_DV), lambda b, h, i: (0, 0))],
        out_specs=pl.BlockSpec((t, GLA_DV), lambda b, h, i: (b * nt + i, h)),
        out_shape=jax.ShapeDtypeStruct((batch * seq, GLA_VAL_DIM), BF16),
        scratch_shapes=[pltpu.VMEM((GLA_DV, GLA_DK), F32),
                        pltpu.VMEM((t, GLA_DV), F32)],
        compiler_params=_params("parallel", "parallel", "arbitrary"),
        name=name,
    )(proj, proj, proj, proj, z, wg, bg, gn)


def kernel(x, attn_w_qkv, attn_w_o, attn_lambda, attn_subln, gla_w_in, gla_w_gk2, gla_b_gk,
           gla_norm, gla_w_o, mlp_w1, mlp_w2, ln_g, ln_b):
    batch, seq, d = x.shape
    m = batch * seq
    h = x.reshape(m, d)
    hb = h.astype(BF16)

    qk_cols = 4 * DIFF_HEADS * DIFF_HEAD_DIM
    q_cols = qk_cols // 2
    qkv_scale = jnp.concatenate(
        [jnp.full((1, q_cols), DIFF_HEAD_DIM ** -0.5, F32),
         jnp.ones((1, attn_w_qkv.shape[-1] - q_cols), F32)], axis=-1)
    main_cols = 2 * GLA_KEY_DIM + 2 * GLA_VAL_DIM
    gla_scale = jnp.concatenate(
        [jnp.full((1, GLA_KEY_DIM), GLA_DK ** -0.5, F32),
         jnp.ones((1, main_cols - GLA_KEY_DIM), F32)], axis=-1)
    ones_lanes = jnp.ones((1, LANES), F32)

    for i in range(DEPTH):
        j = i // 2
        if i % 2 == 0:
            lam_init = 0.8 - 0.6 * math.exp(-0.3 * i)
            qkv = _proj(hb, attn_w_qkv[j].astype(BF16), qkv_scale,
                        tm=1024, tn=1024, name=f"qkv_proj_{i}")
            y = _diff_attention(qkv, attn_lambda[j], attn_subln[j].reshape(1, -1),
                                batch=batch, seq=seq, lam_init=lam_init, t=512,
                                name=f"diff_attn_{i}")
            w_o = attn_w_o[j]
        else:
            w_in = gla_w_in[j]
            w_z = jnp.pad(w_in[:, main_cols:], ((0, 0), (0, LANES - GLA_GATE_RANK)))
            proj = _proj(hb, w_in[:, :main_cols].astype(BF16), gla_scale,
                         tm=1024, tn=1024, name=f"gla_proj_{i}")
            z = _proj(hb, w_z.astype(BF16), ones_lanes, tm=1024, tn=LANES,
                      name=f"gla_gate_proj_{i}")
            wg = jnp.pad(gla_w_gk2[j], ((0, LANES - GLA_GATE_RANK), (0, 0))).astype(BF16)
            y = _gla(proj, z, wg, gla_b_gk[j].reshape(1, -1), gla_norm[j].reshape(1, -1),
                     batch=batch, seq=seq, t=512, name=f"gla_{i}")
            w_o = gla_w_o[j]
        h, hb = _out_ln(y, w_o.astype(BF16), h, ln_g[i, 0].reshape(1, -1),
                        ln_b[i, 0].reshape(1, -1), tm=256, name=f"mixer_out_ln_{i}")
        h, hb = _mlp(hb, mlp_w1[i].astype(BF16), mlp_w2[i].astype(BF16), h,
                     ln_g[i, 1].reshape(1, -1), ln_b[i, 1].reshape(1, -1),
                     tm=512, tf=512, name=f"mlp_ln_{i}")
    return h.reshape(batch, seq, d)
```

```python
import functools
import math

import jax
import jax.numpy as jnp
from jax import lax
from jax.experimental import pallas as pl
from jax.experimental.pallas import tpu as pltpu

D_MODEL = 2048
DEPTH = 4
CHUNK = 64

DIFF_HEAD_DIM = 128
DIFF_HEADS = D_MODEL // (2 * DIFF_HEAD_DIM)

GLA_HEADS = 4
GLA_KEY_DIM = D_MODEL // 2
GLA_VAL_DIM = D_MODEL
GLA_DK = GLA_KEY_DIM // GLA_HEADS
GLA_DV = GLA_VAL_DIM // GLA_HEADS
GLA_GATE_RANK = 16
GLA_GATE_TAU = 16.0

D_FF = 4 * D_MODEL

DN_ALPHA = (2.0 * DEPTH) ** 0.25
LN_EPS = 1e-5
RMS_EPS = 1e-6

ATTN_BLOCK = 512
SUM_ROWS = 16
LANES = 128
VMEM_LIMIT = 56 * 1024 * 1024

F32 = jnp.float32
BF16 = jnp.bfloat16

_NT = (((1,), (1,)), ((), ()))
_TN = (((0,), (0,)), ((), ()))


def _params(*sem):
    return pltpu.CompilerParams(dimension_semantics=sem, vmem_limit_bytes=VMEM_LIMIT)


def _layer_norm(xf, g, b):
    mu = jnp.mean(xf, axis=-1, keepdims=True)
    xc = xf - mu
    var = jnp.mean(xc * xc, axis=-1, keepdims=True)
    return xc * lax.rsqrt(var + LN_EPS) * g + b


def _proj_kernel(x_ref, w_ref, s_ref, o_ref):
    acc = jnp.dot(x_ref[...], w_ref[...], preferred_element_type=F32)
    o_ref[...] = (acc * s_ref[...]).astype(o_ref.dtype)


def _proj(x, w, colscale, *, tm, tn, name, n=None):
    m, k = x.shape
    n = w.shape[1] if n is None else n
    return pl.pallas_call(
        _proj_kernel,
        grid=(m // tm, n // tn),
        in_specs=[pl.BlockSpec((tm, k), lambda i, j: (i, 0)),
                  pl.BlockSpec((k, tn), lambda i, j: (0, j)),
                  pl.BlockSpec((1, tn), lambda i, j: (0, j))],
        out_specs=pl.BlockSpec((tm, tn), lambda i, j: (i, j)),
        out_shape=jax.ShapeDtypeStruct((m, n), BF16),
        compiler_params=_params("parallel", "parallel"),
        name=name,
    )(x, w, colscale)


def _out_ln_kernel(a_ref, w_ref, h_ref, g_ref, b_ref, of_ref, ob_ref):
    y = jnp.dot(a_ref[...], w_ref[...], preferred_element_type=F32)
    hn = _layer_norm(DN_ALPHA * h_ref[...] + y, g_ref[...], b_ref[...])
    of_ref[...] = hn
    ob_ref[...] = hn.astype(BF16)


def _out_ln(a, w, h, g, b, *, tm, name):
    m, k = a.shape
    d = w.shape[1]
    row = lambda i: (i, 0)
    const = lambda i: (0, 0)
    return pl.pallas_call(
        _out_ln_kernel,
        grid=(m // tm,),
        in_specs=[pl.BlockSpec((tm, k), row),
                  pl.BlockSpec((k, d), const),
                  pl.BlockSpec((tm, d), row),
                  pl.BlockSpec((1, d), const),
                  pl.BlockSpec((1, d), const)],
        out_specs=[pl.BlockSpec((tm, d), row), pl.BlockSpec((tm, d), row)],
        out_shape=[jax.ShapeDtypeStruct((m, d), F32),
                   jax.ShapeDtypeStruct((m, d), BF16)],
        compiler_params=_params("parallel"),
        name=name,
    )(a, w, h, g, b)


def _mlp_kernel(x_ref, w1_ref, w2_ref, h_ref, g_ref, b_ref, of_ref, ob_ref, acc_ref):
    f = pl.program_id(1)

    @pl.when(f == 0)
    def _():
        acc_ref[...] = jnp.zeros_like(acc_ref)

    u = jnp.dot(x_ref[...], w1_ref[...], preferred_element_type=F32)
    u = jnp.maximum(u, 0.0)
    acc_ref[...] += jnp.dot((u * u).astype(BF16), w2_ref[...], preferred_element_type=F32)

    @pl.when(f == pl.num_programs(1) - 1)
    def _():
        hn = _layer_norm(DN_ALPHA * h_ref[...] + acc_ref[...], g_ref[...], b_ref[...])
        of_ref[...] = hn
        ob_ref[...] = hn.astype(BF16)


def _mlp(x, w1, w2, h, g, b, *, tm, tf, name):
    m, d = x.shape
    ff = w1.shape[1]
    row = lambda i, f: (i, 0)
    const = lambda i, f: (0, 0)
    return pl.pallas_call(
        _mlp_kernel,
        grid=(m // tm, ff // tf),
        in_specs=[pl.BlockSpec((tm, d), row),
                  pl.BlockSpec((d, tf), lambda i, f: (0, f)),
                  pl.BlockSpec((tf, d), lambda i, f: (f, 0)),
                  pl.BlockSpec((tm, d), row),
                  pl.BlockSpec((1, d), const),
                  pl.BlockSpec((1, d), const)],
        out_specs=[pl.BlockSpec((tm, d), row), pl.BlockSpec((tm, d), row)],
        out_shape=[jax.ShapeDtypeStruct((m, d), F32),
                   jax.ShapeDtypeStruct((m, d), BF16)],
        scratch_shapes=[pltpu.VMEM((tm, d), F32)],
        compiler_params=_params("parallel", "arbitrary"),
        name=name,
    )(x, w1, w2, h, g, b)


def _proj_t_kernel(wt_ref, x_ref, o_ref):
    o_ref[0] = lax.dot_general(wt_ref[...], x_ref[...], _NT,
                               preferred_element_type=F32).astype(o_ref.dtype)


def _proj_t(wt, x, *, tm, name):
    n, k = wt.shape
    m = x.shape[0]
    return pl.pallas_call(
        _proj_t_kernel,
        grid=(m // tm,),
        in_specs=[pl.BlockSpec((n, k), lambda i: (0, 0)),
                  pl.BlockSpec((tm, k), lambda i: (i, 0))],
        out_specs=pl.BlockSpec((1, n, tm), lambda i: (i, 0, 0)),
        out_shape=jax.ShapeDtypeStruct((m // tm, n, tm), BF16),
        compiler_params=_params("parallel"),
        name=name,
    )(wt, x)


def _attn_kernel(lam_ref, sub_ref, q_ref, k_ref, vt_ref, o_ref,
                 s0_ref, s1_ref, p0_ref, p1_ref, acc0_ref, acc1_ref, *, lam_init, t):
    d = DIFF_HEAD_DIM
    i = pl.program_id(2)
    s_ref, p_ref, acc_ref = (s0_ref, s1_ref), (p0_ref, p1_ref), (acc0_ref, acc1_ref)

    lam = lam_ref[...]
    lam_full = (jnp.exp(jnp.sum(lam[0:1] * lam[1:2], keepdims=True))
                - jnp.exp(jnp.sum(lam[2:3] * lam[3:4], keepdims=True)) + lam_init)

    key_chunk = lax.broadcasted_iota(jnp.int32, (t, t), 0) // CHUNK
    qry_chunk = lax.broadcasted_iota(jnp.int32, (t, t), 1) // CHUNK
    allowed = key_chunk <= qry_chunk

    def scores(a, j):
        off = pl.multiple_of(j * t, t)
        s_ref[a][...] = lax.dot_general(k_ref[pl.ds(off, t), a * d:(a + 1) * d],
                                        q_ref[:, a * d:(a + 1) * d], _NT,
                                        preferred_element_type=F32)

    def softmax(a, m, masked):
        st = s_ref[a][...]
        if masked:
            st = jnp.where(allowed, st, -jnp.inf)
        mn = jnp.maximum(m, jnp.max(st, axis=0, keepdims=True))
        p_ref[a][...] = jnp.exp2(st - mn).astype(BF16)
        return mn, jnp.exp2(m - mn)

    ones_rows = jnp.ones((SUM_ROWS, t), BF16)

    def weighted_values(a, j, c):
        vt1 = jnp.concatenate([vt_ref[j], ones_rows], axis=0)
        acc_ref[a][...] = c * acc_ref[a][...] + jnp.dot(vt1, p_ref[a][...],
                                                         preferred_element_type=F32)

    def step(j, carry):
        m0, m1, c1 = carry
        scores(1, j)
        m0, c0 = softmax(0, m0, masked=False)
        weighted_values(1, jnp.maximum(j - 1, 0), c1)
        scores(0, j + 1)
        m1, c1 = softmax(1, m1, masked=False)
        weighted_values(0, j, c0)
        return m0, m1, c1

    acc0_ref[...] = jnp.zeros_like(acc0_ref)
    acc1_ref[...] = jnp.zeros_like(acc1_ref)
    p1_ref[...] = jnp.zeros_like(p1_ref)
    scores(0, 0)
    neg = jnp.full((1, t), -jnp.inf, F32)
    m0, m1, c1 = lax.fori_loop(0, i, step, (neg, neg, jnp.ones((1, t), F32)))

    scores(1, i)
    m0, c0 = softmax(0, m0, masked=True)
    weighted_values(1, jnp.maximum(i - 1, 0), c1)
    m1, c1 = softmax(1, m1, masked=True)
    weighted_values(0, i, c0)
    weighted_values(1, i, c1)

    h2d = 2 * d
    ot = (acc0_ref[:h2d, :] / acc0_ref[h2d:h2d + 1, :]
          - lam_full * (acc1_ref[:h2d, :] / acc1_ref[h2d:h2d + 1, :]))
    ms = jnp.mean(ot * ot, axis=0, keepdims=True)
    o = jnp.transpose(ot * lax.rsqrt(ms + RMS_EPS))
    o_ref[...] = (o * sub_ref[...] * (1.0 - lam_init)).astype(o_ref.dtype)


def _diff_attention(qk, vt, lam, subln, *, batch, seq, lam_init, t, name):
    h2d = 2 * DIFF_HEAD_DIM
    nh = DIFF_HEADS
    nq = seq // t
    return pl.pallas_call(
        functools.partial(_attn_kernel, lam_init=lam_init, t=t),
        grid=(batch, nh, nq),
        in_specs=[pl.BlockSpec((4, DIFF_HEAD_DIM), lambda b, h, i: (0, 0)),
                  pl.BlockSpec((1, h2d), lambda b, h, i: (0, 0)),
                  pl.BlockSpec((t, h2d), lambda b, h, i: (b * nq + i, h)),
                  pl.BlockSpec((seq, h2d), lambda b, h, i: (b, nh + h)),
                  pl.BlockSpec((nq, h2d, t), lambda b, h, i: (b, h, 0))],
        out_specs=pl.BlockSpec((t, h2d), lambda b, h, i: (b * nq + i, h)),
        out_shape=jax.ShapeDtypeStruct((batch * seq, nh * h2d), BF16),
        scratch_shapes=[pltpu.VMEM((t, t), F32), pltpu.VMEM((t, t), F32),
                        pltpu.VMEM((t, t), BF16), pltpu.VMEM((t, t), BF16),
                        pltpu.VMEM((h2d + SUM_ROWS, t), F32),
                        pltpu.VMEM((h2d + SUM_ROWS, t), F32)],
        compiler_params=_params("parallel", "parallel", "arbitrary"),
        name=name,
    )(lam, subln, qk, qk, vt)


def _gla_kernel(q_ref, k_ref, v_ref, r_ref, z_ref, wg_ref, bg_ref, gn_ref, o_ref,
                st_ref, oacc_ref, *, t):
    @pl.when(pl.program_id(2) == 0)
    def _():
        st_ref[...] = jnp.zeros_like(st_ref)

    gk = jnp.dot(z_ref[...], wg_ref[...], preferred_element_type=F32) + bg_ref[...]
    log_a = jax.nn.log_sigmoid(gk) / GLA_GATE_TAU

    pos = lax.broadcasted_iota(jnp.int32, (t, GLA_DK), 0) % CHUNK
    b = log_a
    shift = 1
    while shift < CHUNK:
        b = b + jnp.where(pos >= shift, pltpu.roll(b, shift, axis=0), 0.0)
        shift *= 2

    tr = lax.broadcasted_iota(jnp.int32, (CHUNK, CHUNK), 0)
    tc = lax.broadcasted_iota(jnp.int32, (CHUNK, CHUNK), 1)
    causal = tc <= tr

    for c in range(t // CHUNK):
        sl = slice(c * CHUNK, (c + 1) * CHUNK)
        bc = b[sl]
        bl = bc[CHUNK - 1:CHUNK]
        qc = q_ref[sl, :].astype(F32)
        kc = k_ref[sl, :].astype(F32)
        vc = v_ref[sl, :]
        q_dec = (qc * jnp.exp(bc)).astype(BF16)
        k_inv = (kc * jnp.exp(-bc)).astype(BF16)
        k_st = (kc * jnp.exp(bl - bc)).astype(BF16)
        attn = lax.dot_general(q_dec, k_inv, _NT, preferred_element_type=F32)
        attn = jnp.where(causal, attn, 0.0).astype(BF16)
        st = st_ref[...]
        oacc_ref[sl, :] = (
            jnp.dot(attn, vc, preferred_element_type=F32)
            + lax.dot_general(q_dec, st.astype(BF16), _NT, preferred_element_type=F32))
        st_ref[...] = jnp.exp(bl) * st + lax.dot_general(
            vc, k_st, _TN, preferred_element_type=F32)

    o = oacc_ref[...]
    ms = jnp.mean(o * o, axis=-1, keepdims=True)
    o = o * lax.rsqrt(ms + RMS_EPS) * gn_ref[...]
    r = r_ref[...].astype(F32)
    o = o * (r * jax.nn.sigmoid(r))
    o_ref[...] = o.astype(o_ref.dtype)


def _gla(proj, z, wg, bg, gn, *, batch, seq, t, name):
    nh = GLA_HEADS
    nt = seq // t
    tok = lambda off: (lambda b, h, i: (b * nt + i, off + h))
    return pl.pallas_call(
        functools.partial(_gla_kernel, t=t),
        grid=(batch, nh, nt),
        in_specs=[pl.BlockSpec((t, GLA_DK), tok(0)),
                  pl.BlockSpec((t, GLA_DK), tok(nh)),
                  pl.BlockSpec((t, GLA_DV), tok(nh)),
                  pl.BlockSpec((t, GLA_DV), tok(2 * nh)),
                  pl.BlockSpec((t, LANES), lambda b, h, i: (b * nt + i, 0)),
                  pl.BlockSpec((LANES, GLA_DK), lambda b, h, i: (0, h)),
                  pl.BlockSpec((1, GLA_DK), lambda b, h, i: (0, h)),
                  pl.BlockSpec((1, GLA_DV), lambda b, h, i: (0, 0))],
        out_specs=pl.BlockSpec((t, GLA_DV), lambda b, h, i: (b * nt + i, h)),
        out_shape=jax.ShapeDtypeStruct((batch * seq, GLA_VAL_DIM), BF16),
        scratch_shapes=[pltpu.VMEM((GLA_DV, GLA_DK), F32),
                        pltpu.VMEM((t, GLA_DV), F32)],
        compiler_params=_params("parallel", "parallel", "arbitrary"),
        name=name,
    )(proj, proj, proj, proj, z, wg, bg, gn)


def kernel(x, attn_w_qkv, attn_w_o, attn_lambda, attn_subln, gla_w_in, gla_w_gk2, gla_b_gk,
           gla_norm, gla_w_o, mlp_w1, mlp_w2, ln_g, ln_b):
    batch, seq, d = x.shape
    m = batch * seq
    h = x.reshape(m, d)
    hb = h.astype(BF16)

    qk_cols = 4 * DIFF_HEADS * DIFF_HEAD_DIM
    q_cols = qk_cols // 2
    qkv_scale = jnp.concatenate(
        [jnp.full((1, q_cols), DIFF_HEAD_DIM ** -0.5 * math.log2(math.e), F32),
         jnp.ones((1, attn_w_qkv.shape[-1] - q_cols), F32)], axis=-1)
    main_cols = 2 * GLA_KEY_DIM + 2 * GLA_VAL_DIM
    gla_scale = jnp.concatenate(
        [jnp.full((1, GLA_KEY_DIM), GLA_DK ** -0.5, F32),
         jnp.ones((1, main_cols - GLA_KEY_DIM), F32)], axis=-1)
    ones_lanes = jnp.ones((1, LANES), F32)

    for i in range(DEPTH):
        j = i // 2
        if i % 2 == 0:
            lam_init = 0.8 - 0.6 * math.exp(-0.3 * i)
            w_qkv = attn_w_qkv[j]
            qk = _proj(hb, w_qkv.astype(BF16), qkv_scale, n=qk_cols,
                       tm=1024, tn=1024, name=f"qk_proj_{i}")
            vt = _proj_t(w_qkv[:, qk_cols:].T.astype(BF16), hb, tm=ATTN_BLOCK,
                         name=f"vt_proj_{i}")
            y = _diff_attention(qk, vt, attn_lambda[j], attn_subln[j].reshape(1, -1),
                                batch=batch, seq=seq, lam_init=lam_init, t=ATTN_BLOCK,
                                name=f"diff_attn_{i}")
            w_o = attn_w_o[j]
        else:
            w_in = gla_w_in[j]
            w_z = jnp.pad(w_in[:, main_cols:], ((0, 0), (0, LANES - GLA_GATE_RANK)))
            proj = _proj(hb, w_in.astype(BF16), gla_scale, n=main_cols,
                         tm=1024, tn=1024, name=f"gla_proj_{i}")
            z = _proj(hb, w_z.astype(BF16), ones_lanes, tm=1024, tn=LANES,
                      name=f"gla_gate_proj_{i}")
            wg = jnp.pad(gla_w_gk2[j], ((0, LANES - GLA_GATE_RANK), (0, 0))).astype(BF16)
            y = _gla(proj, z, wg, gla_b_gk[j].reshape(1, -1), gla_norm[j].reshape(1, -1),
                     batch=batch, seq=seq, t=512, name=f"gla_{i}")
            w_o = gla_w_o[j]
        h, hb = _out_ln(y, w_o.astype(BF16), h, ln_g[i, 0].reshape(1, -1),
                        ln_b[i, 0].reshape(1, -1), tm=256, name=f"mixer_out_ln_{i}")
        h, hb = _mlp(hb, mlp_w1[i].astype(BF16), mlp_w2[i].astype(BF16), h,
                     ln_g[i, 1].reshape(1, -1), ln_b[i, 1].reshape(1, -1),
                     tm=512, tf=1024, name=f"mlp_ln_{i}")
    return h.reshape(batch, seq, d)
```

```python
import functools
import math

import jax
import jax.numpy as jnp
from jax import lax
from jax.experimental import pallas as pl
from jax.experimental.pallas import tpu as pltpu

D_MODEL = 2048
DEPTH = 4
CHUNK = 64

DIFF_HEAD_DIM = 128
DIFF_HEADS = D_MODEL // (2 * DIFF_HEAD_DIM)

GLA_HEADS = 4
GLA_KEY_DIM = D_MODEL // 2
GLA_VAL_DIM = D_MODEL
GLA_DK = GLA_KEY_DIM // GLA_HEADS
GLA_DV = GLA_VAL_DIM // GLA_HEADS
GLA_GATE_RANK = 16
GLA_GATE_TAU = 16.0

D_FF = 4 * D_MODEL

DN_ALPHA = (2.0 * DEPTH) ** 0.25
LN_EPS = 1e-5
RMS_EPS = 1e-6

ATTN_BLOCK = 512
SUM_ROWS = 16
LANES = 128
VMEM_LIMIT = 56 * 1024 * 1024

F32 = jnp.float32
BF16 = jnp.bfloat16

_NT = (((1,), (1,)), ((), ()))
_TN = (((0,), (0,)), ((), ()))


def _params(*sem):
    return pltpu.CompilerParams(dimension_semantics=sem, vmem_limit_bytes=VMEM_LIMIT)


def _layer_norm(xf, g, b):
    mu = jnp.mean(xf, axis=-1, keepdims=True)
    xc = xf - mu
    var = jnp.mean(xc * xc, axis=-1, keepdims=True)
    return xc * lax.rsqrt(var + LN_EPS) * g + b


def _proj_kernel(x_ref, w_ref, s_ref, o_ref, wb_ref):
    @pl.when(pl.program_id(1) == 0)
    def _():
        wb_ref[...] = w_ref[...].astype(BF16)

    acc = jnp.dot(x_ref[...].astype(BF16), wb_ref[...], preferred_element_type=F32)
    o_ref[...] = (acc * s_ref[...]).astype(o_ref.dtype)


def _proj(x, w, layer, colscale, *, tm, tn, name, n=None):
    m, k = x.shape
    n = w.shape[2] if n is None else n
    return pl.pallas_call(
        _proj_kernel,
        grid=(n // tn, m // tm),
        in_specs=[pl.BlockSpec((tm, k), lambda j, i: (i, 0)),
                  pl.BlockSpec((None, k, tn), lambda j, i: (layer, 0, j)),
                  pl.BlockSpec((1, tn), lambda j, i: (0, j))],
        out_specs=pl.BlockSpec((tm, tn), lambda j, i: (i, j)),
        out_shape=jax.ShapeDtypeStruct((m, n), BF16),
        scratch_shapes=[pltpu.VMEM((k, tn), BF16)],
        compiler_params=_params("parallel", "arbitrary"),
        name=name,
    )(x, w, colscale)


def _out_ln_kernel(a_ref, w_ref, h_ref, g_ref, b_ref, of_ref, ob_ref, *, sub):
    for s in range(a_ref.shape[0] // sub):
        rows = slice(s * sub, (s + 1) * sub)
        y = jnp.dot(a_ref[rows, :], w_ref[...], preferred_element_type=F32)
        hn = _layer_norm(DN_ALPHA * h_ref[rows, :] + y, g_ref[...], b_ref[...])
        of_ref[rows, :] = hn
        ob_ref[rows, :] = hn.astype(BF16)


def _out_ln(a, w, layer, h, g, b, *, tm, sub, name):
    m, k = a.shape
    d = w.shape[2]
    row = lambda i: (i, 0)
    const = lambda i: (0, 0)
    return pl.pallas_call(
        functools.partial(_out_ln_kernel, sub=sub),
        grid=(m // tm,),
        in_specs=[pl.BlockSpec((tm, k), row),
                  pl.BlockSpec((None, k, d), lambda i: (layer, 0, 0)),
                  pl.BlockSpec((tm, d), row),
                  pl.BlockSpec((1, d), const),
                  pl.BlockSpec((1, d), const)],
        out_specs=[pl.BlockSpec((tm, d), row), pl.BlockSpec((tm, d), row)],
        out_shape=[jax.ShapeDtypeStruct((m, d), F32),
                   jax.ShapeDtypeStruct((m, d), BF16)],
        compiler_params=_params("parallel"),
        name=name,
    )(a, w, h, g, b)


def _mlp_kernel(x_ref, w1_ref, w2_ref, h_ref, g_ref, b_ref, of_ref, ob_ref, acc_ref):
    f = pl.program_id(1)

    @pl.when(f == 0)
    def _():
        acc_ref[...] = jnp.zeros_like(acc_ref)

    u = jnp.dot(x_ref[...], w1_ref[...], preferred_element_type=F32)
    u = jnp.maximum(u, 0.0)
    acc_ref[...] += jnp.dot((u * u).astype(BF16), w2_ref[...], preferred_element_type=F32)

    @pl.when(f == pl.num_programs(1) - 1)
    def _():
        hn = _layer_norm(DN_ALPHA * h_ref[...] + acc_ref[...], g_ref[...], b_ref[...])
        of_ref[...] = hn
        ob_ref[...] = hn.astype(BF16)


def _mlp(x, w1, w2, layer, h, g, b, *, tm, tf, name):
    m, d = x.shape
    ff = w1.shape[2]
    row = lambda i, f: (i, 0)
    const = lambda i, f: (0, 0)
    return pl.pallas_call(
        _mlp_kernel,
        grid=(m // tm, ff // tf),
        in_specs=[pl.BlockSpec((tm, d), row),
                  pl.BlockSpec((None, d, tf), lambda i, f: (layer, 0, f)),
                  pl.BlockSpec((None, tf, d), lambda i, f: (layer, f, 0)),
                  pl.BlockSpec((tm, d), row),
                  pl.BlockSpec((1, d), const),
                  pl.BlockSpec((1, d), const)],
        out_specs=[pl.BlockSpec((tm, d), row), pl.BlockSpec((tm, d), row)],
        out_shape=[jax.ShapeDtypeStruct((m, d), F32),
                   jax.ShapeDtypeStruct((m, d), BF16)],
        scratch_shapes=[pltpu.VMEM((tm, d), F32)],
        compiler_params=_params("parallel", "arbitrary"),
        name=name,
    )(x, w1, w2, h, g, b)


def _proj_t_kernel(w_ref, x_ref, o_ref, wb_ref):
    @pl.when(pl.program_id(0) == 0)
    def _():
        wb_ref[...] = w_ref[...].astype(BF16)

    y = jnp.dot(x_ref[...].astype(BF16), wb_ref[...], preferred_element_type=F32)
    o_ref[0] = jnp.transpose(y).astype(o_ref.dtype)


def _proj_t(w, layer, col_block, n, x, *, tm, name):
    k = w.shape[1]
    m = x.shape[0]
    return pl.pallas_call(
        _proj_t_kernel,
        grid=(m // tm,),
        in_specs=[pl.BlockSpec((None, k, n), lambda i: (layer, 0, col_block),
                               pipeline_mode=pl.Buffered(1)),
                  pl.BlockSpec((tm, k), lambda i: (i, 0))],
        out_specs=pl.BlockSpec((1, n, tm), lambda i: (i, 0, 0)),
        out_shape=jax.ShapeDtypeStruct((m // tm, n, tm), BF16),
        scratch_shapes=[pltpu.VMEM((k, n), BF16)],
        compiler_params=_params("arbitrary"),
        name=name,
    )(w, x)


def _attn_kernel(lam_ref, sub_ref, q_ref, k_ref, vt_ref, o_ref,
                 s0_ref, s1_ref, p0_ref, p1_ref, acc0_ref, acc1_ref, *, lam_init, t):
    d = DIFF_HEAD_DIM
    i = pl.program_id(2)
    s_ref, p_ref, acc_ref = (s0_ref, s1_ref), (p0_ref, p1_ref), (acc0_ref, acc1_ref)

    lam = lam_ref[...]
    lam_full = (jnp.exp(jnp.sum(lam[0:1] * lam[1:2], keepdims=True))
                - jnp.exp(jnp.sum(lam[2:3] * lam[3:4], keepdims=True)) + lam_init)

    key_chunk = lax.broadcasted_iota(jnp.int32, (t, t), 0) // CHUNK
    qry_chunk = lax.broadcasted_iota(jnp.int32, (t, t), 1) // CHUNK
    allowed = key_chunk <= qry_chunk

    def scores(a, j):
        off = pl.multiple_of(j * t, t)
        s_ref[a][:, :t] = lax.dot_general(k_ref[pl.ds(off, t), a * d:(a + 1) * d],
                                          q_ref[:, a * d:(a + 1) * d], _NT,
                                          preferred_element_type=F32)

    def softmax(a, m, masked):
        st = s_ref[a][:, :t]
        if masked:
            st = jnp.where(allowed, st, -jnp.inf)
        mn = jnp.maximum(m, jnp.max(st, axis=0, keepdims=True))
        p_ref[a][:, :t] = jnp.exp2(st - mn).astype(BF16)
        return mn, jnp.exp2(m - mn)

    ones_rows = jnp.ones((SUM_ROWS, t), BF16)

    def weighted_values(a, j, c):
        vt1 = jnp.concatenate([vt_ref[j], ones_rows], axis=0)
        acc_ref[a][...] = c * acc_ref[a][...] + jnp.dot(vt1, p_ref[a][:, :t],
                                                         preferred_element_type=F32)

    def step(j, carry):
        m0, m1, c1 = carry
        scores(1, j)
        m0, c0 = softmax(0, m0, masked=False)
        weighted_values(1, jnp.maximum(j - 1, 0), c1)
        scores(0, j + 1)
        m1, c1 = softmax(1, m1, masked=False)
        weighted_values(0, j, c0)
        return m0, m1, c1

    acc0_ref[...] = jnp.zeros_like(acc0_ref)
    acc1_ref[...] = jnp.zeros_like(acc1_ref)
    p1_ref[:, :t] = jnp.zeros((t, t), BF16)
    scores(0, 0)
    neg = jnp.full((1, t), -jnp.inf, F32)
    m0, m1, c1 = lax.fori_loop(0, i, step, (neg, neg, jnp.ones((1, t), F32)))

    scores(1, i)
    m0, c0 = softmax(0, m0, masked=True)
    weighted_values(1, jnp.maximum(i - 1, 0), c1)
    m1, c1 = softmax(1, m1, masked=True)
    weighted_values(0, i, c0)
    weighted_values(1, i, c1)

    h2d = 2 * d
    ot = (acc0_ref[:h2d, :] / acc0_ref[h2d:h2d + 1, :]
          - lam_full * (acc1_ref[:h2d, :] / acc1_ref[h2d:h2d + 1, :]))
    ms = jnp.mean(ot * ot, axis=0, keepdims=True)
    o = jnp.transpose(ot * lax.rsqrt(ms + RMS_EPS))
    o_ref[...] = (o * sub_ref[...] * (1.0 - lam_init)).astype(o_ref.dtype)


def _diff_attention(qk, vt, lam, subln, *, batch, seq, lam_init, t, name):
    h2d = 2 * DIFF_HEAD_DIM
    nh = DIFF_HEADS
    nq = seq // t
    return pl.pallas_call(
        functools.partial(_attn_kernel, lam_init=lam_init, t=t),
        grid=(batch, nh, nq),
        in_specs=[pl.BlockSpec((4, DIFF_HEAD_DIM), lambda b, h, i: (0, 0)),
                  pl.BlockSpec((1, h2d), lambda b, h, i: (0, 0)),
                  pl.BlockSpec((t, h2d), lambda b, h, i: (b * nq + i, h)),
                  pl.BlockSpec((seq, h2d), lambda b, h, i: (b, nh + h)),
                  pl.BlockSpec((nq, h2d, t), lambda b, h, i: (b, h, 0))],
        out_specs=pl.BlockSpec((t, h2d), lambda b, h, i: (b * nq + i, h)),
        out_shape=jax.ShapeDtypeStruct((batch * seq, nh * h2d), BF16),
        scratch_shapes=[pltpu.VMEM((t, t + LANES), F32), pltpu.VMEM((t, t + LANES), F32),
                        pltpu.VMEM((t, t + LANES), BF16), pltpu.VMEM((t, t + LANES), BF16),
                        pltpu.VMEM((h2d + SUM_ROWS, t), F32),
                        pltpu.VMEM((h2d + SUM_ROWS, t), F32)],
        compiler_params=_params("parallel", "parallel", "arbitrary"),
        name=name,
    )(lam, subln, qk, qk, vt)


def _gla_kernel(q_ref, k_ref, v_ref, r_ref, z_ref, wg_ref, bg_ref, gn_ref, o_ref,
                st_ref, oacc_ref, *, t):
    @pl.when(pl.program_id(2) == 0)
    def _():
        st_ref[...] = jnp.zeros_like(st_ref)

    gk = jnp.dot(z_ref[...], wg_ref[...], preferred_element_type=F32) + bg_ref[...]
    log_a = jax.nn.log_sigmoid(gk) / GLA_GATE_TAU

    pos = lax.broadcasted_iota(jnp.int32, (t, GLA_DK), 0) % CHUNK
    b = log_a
    shift = 1
    while shift < CHUNK:
        b = b + jnp.where(pos >= shift, pltpu.roll(b, shift, axis=0), 0.0)
        shift *= 2

    tr = lax.broadcasted_iota(jnp.int32, (CHUNK, CHUNK), 0)
    tc = lax.broadcasted_iota(jnp.int32, (CHUNK, CHUNK), 1)
    causal = tc <= tr

    for c in range(t // CHUNK):
        sl = slice(c * CHUNK, (c + 1) * CHUNK)
        bc = b[sl]
        bl = bc[CHUNK - 1:CHUNK]
        qc = q_ref[sl, :].astype(F32)
        kc = k_ref[sl, :].astype(F32)
        vc = v_ref[sl, :]
        q_dec = (qc * jnp.exp(bc)).astype(BF16)
        k_inv = (kc * jnp.exp(-bc)).astype(BF16)
        k_st = (kc * jnp.exp(bl - bc)).astype(BF16)
        attn = lax.dot_general(q_dec, k_inv, _NT, preferred_element_type=F32)
        attn = jnp.where(causal, attn, 0.0).astype(BF16)
        st = st_ref[...]
        oacc_ref[sl, :] = (
            jnp.dot(attn, vc, preferred_element_type=F32)
            + lax.dot_general(q_dec, st.astype(BF16), _NT, preferred_element_type=F32))
        st_ref[...] = jnp.exp(bl) * st + lax.dot_general(
            vc, k_st, _TN, preferred_element_type=F32)

    o = oacc_ref[...]
    ms = jnp.mean(o * o, axis=-1, keepdims=True)
    o = o * lax.rsqrt(ms + RMS_EPS) * gn_ref[...]
    r = r_ref[...].astype(F32)
    o = o * (r * jax.nn.sigmoid(r))
    o_ref[...] = o.astype(o_ref.dtype)


def _gla(proj, z, wg, bg, gn, *, batch, seq, t, name):
    nh = GLA_HEADS
    nt = seq // t
    tok = lambda off: (lambda b, h, i: (b * nt + i, off + h))
    return pl.pallas_call(
        functools.partial(_gla_kernel, t=t),
        grid=(batch, nh, nt),
        in_specs=[pl.BlockSpec((t, GLA_DK), tok(0)),
                  pl.BlockSpec((t, GLA_DK), tok(nh)),
                  pl.BlockSpec((t, GLA_DV), tok(nh)),
                  pl.BlockSpec((t, GLA_DV), tok(2 * nh)),
                  pl.BlockSpec((t, LANES), lambda b, h, i: (b * nt + i, 0)),
                  pl.BlockSpec((LANES, GLA_DK), lambda b, h, i: (0, h)),
                  pl.BlockSpec((1, GLA_DK), lambda b, h, i: (0, h)),
                  pl.BlockSpec((1, GLA_DV), lambda b, h, i: (0, 0))],
        out_specs=pl.BlockSpec((t, GLA_DV), lambda b, h, i: (b * nt + i, h)),
        out_shape=jax.ShapeDtypeStruct((batch * seq, GLA_VAL_DIM), BF16),
        scratch_shapes=[pltpu.VMEM((GLA_DV, GLA_DK), F32),
                        pltpu.VMEM((t, GLA_DV), F32)],
        compiler_params=_params("parallel", "parallel", "arbitrary"),
        name=name,
    )(proj, proj, proj, proj, z, wg, bg, gn)


def kernel(x, attn_w_qkv, attn_w_o, attn_lambda, attn_subln, gla_w_in, gla_w_gk2, gla_b_gk,
           gla_norm, gla_w_o, mlp_w1, mlp_w2, ln_g, ln_b):
    batch, seq, d = x.shape
    m = batch * seq
    h = x.reshape(m, d)
    hb = h

    qk_cols = 4 * DIFF_HEADS * DIFF_HEAD_DIM
    q_cols = qk_cols // 2
    qkv_scale = jnp.concatenate(
        [jnp.full((1, q_cols), DIFF_HEAD_DIM ** -0.5 * math.log2(math.e), F32),
         jnp.ones((1, attn_w_qkv.shape[-1] - q_cols), F32)], axis=-1)
    main_cols = 2 * GLA_KEY_DIM + 2 * GLA_VAL_DIM
    gla_scale = jnp.concatenate(
        [jnp.full((1, GLA_KEY_DIM), GLA_DK ** -0.5, F32),
         jnp.ones((1, main_cols - GLA_KEY_DIM), F32)], axis=-1)
    ones_lanes = jnp.ones((1, LANES), F32)

    w_qkv, w_in = attn_w_qkv, gla_w_in
    v_cols = w_qkv.shape[2] - qk_cols
    w_attn_o = attn_w_o.astype(BF16)
    w_z = jnp.pad(gla_w_in[:, :, main_cols:],
                  ((0, 0), (0, 0), (0, LANES - GLA_GATE_RANK))).astype(BF16)
    w_gk2 = jnp.pad(gla_w_gk2, ((0, 0), (0, LANES - GLA_GATE_RANK), (0, 0))).astype(BF16)
    w_gla_o = gla_w_o.astype(BF16)
    w1 = mlp_w1.astype(BF16)
    w2 = mlp_w2.astype(BF16)

    for i in range(DEPTH):
        j = i // 2
        if i % 2 == 0:
            lam_init = 0.8 - 0.6 * math.exp(-0.3 * i)
            qk = _proj(hb, w_qkv, j, qkv_scale, n=qk_cols, tm=1024, tn=1024,
                       name=f"qk_proj_{i}")
            vt = _proj_t(w_qkv, j, qk_cols // v_cols, v_cols, hb, tm=ATTN_BLOCK,
                         name=f"vt_proj_{i}")
            y = _diff_attention(qk, vt, attn_lambda[j], attn_subln[j].reshape(1, -1),
                                batch=batch, seq=seq, lam_init=lam_init, t=ATTN_BLOCK,
                                name=f"diff_attn_{i}")
            w_o = w_attn_o
        else:
            proj = _proj(hb, w_in, j, gla_scale, n=main_cols, tm=1024, tn=1024,
                         name=f"gla_proj_{i}")
            z = _proj(hb, w_z, j, ones_lanes, tm=1024, tn=LANES, name=f"gla_gate_proj_{i}")
            y = _gla(proj, z, w_gk2[j], gla_b_gk[j].reshape(1, -1), gla_norm[j].reshape(1, -1),
                     batch=batch, seq=seq, t=512, name=f"gla_{i}")
            w_o = w_gla_o
        h, hb = _out_ln(y, w_o, j, h, ln_g[i, 0].reshape(1, -1), ln_b[i, 0].reshape(1, -1),
                        tm=512, sub=256, name=f"mixer_out_ln_{i}")
        h, hb = _mlp(hb, w1, w2, i, h, ln_g[i, 1].reshape(1, -1), ln_b[i, 1].reshape(1, -1),
                     tm=512, tf=1024, name=f"mlp_ln_{i}")
    return h.reshape(batch, seq, d)
```

```python
import functools
import math

import jax
import jax.numpy as jnp
from jax import lax
from jax.experimental import pallas as pl
from jax.experimental.pallas import tpu as pltpu

D_MODEL = 2048
DEPTH = 4
CHUNK = 64

DIFF_HEAD_DIM = 128
DIFF_HEADS = D_MODEL // (2 * DIFF_HEAD_DIM)

GLA_HEADS = 4
GLA_KEY_DIM = D_MODEL // 2
GLA_VAL_DIM = D_MODEL
GLA_DK = GLA_KEY_DIM // GLA_HEADS
GLA_DV = GLA_VAL_DIM // GLA_HEADS
GLA_GATE_RANK = 16
GLA_GATE_TAU = 16.0

D_FF = 4 * D_MODEL

DN_ALPHA = (2.0 * DEPTH) ** 0.25
LN_EPS = 1e-5
RMS_EPS = 1e-6

ATTN_BLOCK = 512
SUM_ROWS = 16
LANES = 128
VMEM_LIMIT = 56 * 1024 * 1024

F32 = jnp.float32
BF16 = jnp.bfloat16

_NT = (((1,), (1,)), ((), ()))
_TN = (((0,), (0,)), ((), ()))


def _params(*sem):
    return pltpu.CompilerParams(dimension_semantics=sem, vmem_limit_bytes=VMEM_LIMIT)


def _layer_norm(xf, g, b):
    mu = jnp.mean(xf, axis=-1, keepdims=True)
    xc = xf - mu
    var = jnp.mean(xc * xc, axis=-1, keepdims=True)
    return xc * lax.rsqrt(var + LN_EPS) * g + b


def _proj_kernel(x_ref, w_ref, s_ref, o_ref, wb_ref, *, w_is_transposed):
    @pl.when(pl.program_id(1) == 0)
    def _():
        wb_ref[...] = w_ref[...].astype(BF16)

    dims = _NT if w_is_transposed else (((1,), (0,)), ((), ()))
    acc = lax.dot_general(x_ref[...].astype(BF16), wb_ref[...], dims,
                          preferred_element_type=F32)
    o_ref[...] = (acc * s_ref[...]).astype(o_ref.dtype)


def _proj(x, w, layer, colscale, *, tm, tn, name, n=None, w_is_transposed=False):
    m, k = x.shape
    if n is None:
        n = w.shape[1] if w_is_transposed else w.shape[2]
    if w_is_transposed:
        w_spec = pl.BlockSpec((None, tn, k), lambda j, i: (layer, j, 0))
    else:
        w_spec = pl.BlockSpec((None, k, tn), lambda j, i: (layer, 0, j))
    return pl.pallas_call(
        functools.partial(_proj_kernel, w_is_transposed=w_is_transposed),
        grid=(n // tn, m // tm),
        in_specs=[pl.BlockSpec((tm, k), lambda j, i: (i, 0)),
                  w_spec,
                  pl.BlockSpec((1, tn), lambda j, i: (0, j))],
        out_specs=pl.BlockSpec((tm, tn), lambda j, i: (i, j)),
        out_shape=jax.ShapeDtypeStruct((m, n), BF16),
        scratch_shapes=[pltpu.VMEM(w_spec.block_shape[1:], BF16)],
        compiler_params=_params("parallel", "arbitrary"),
        name=name,
    )(x, w, colscale)


def _out_ln_kernel(a_ref, w_ref, h_ref, g_ref, b_ref, of_ref, ob_ref, *, sub):
    for s in range(a_ref.shape[0] // sub):
        rows = slice(s * sub, (s + 1) * sub)
        y = jnp.dot(a_ref[rows, :], w_ref[...], preferred_element_type=F32)
        hn = _layer_norm(DN_ALPHA * h_ref[rows, :] + y, g_ref[...], b_ref[...])
        of_ref[rows, :] = hn
        ob_ref[rows, :] = hn.astype(BF16)


def _out_ln(a, w, layer, h, g, b, *, tm, sub, name):
    m, k = a.shape
    d = w.shape[2]
    row = lambda i: (i, 0)
    const = lambda i: (0, 0)
    return pl.pallas_call(
        functools.partial(_out_ln_kernel, sub=sub),
        grid=(m // tm,),
        in_specs=[pl.BlockSpec((tm, k), row),
                  pl.BlockSpec((None, k, d), lambda i: (layer, 0, 0)),
                  pl.BlockSpec((tm, d), row),
                  pl.BlockSpec((1, d), const),
                  pl.BlockSpec((1, d), const)],
        out_specs=[pl.BlockSpec((tm, d), row), pl.BlockSpec((tm, d), row)],
        out_shape=[jax.ShapeDtypeStruct((m, d), F32),
                   jax.ShapeDtypeStruct((m, d), BF16)],
        compiler_params=_params("parallel"),
        name=name,
    )(a, w, h, g, b)


def _mlp_kernel(x_ref, w1_ref, w2_ref, h_ref, g_ref, b_ref, of_ref, ob_ref, acc_ref):
    f = pl.program_id(1)

    @pl.when(f == 0)
    def _():
        acc_ref[...] = jnp.zeros_like(acc_ref)

    u = jnp.dot(x_ref[...], w1_ref[...], preferred_element_type=F32)
    u = jnp.maximum(u, 0.0)
    acc_ref[...] += jnp.dot((u * u).astype(BF16), w2_ref[...], preferred_element_type=F32)

    @pl.when(f == pl.num_programs(1) - 1)
    def _():
        hn = _layer_norm(DN_ALPHA * h_ref[...] + acc_ref[...], g_ref[...], b_ref[...])
        of_ref[...] = hn
        ob_ref[...] = hn.astype(BF16)


def _mlp(x, w1, w2, layer, h, g, b, *, tm, tf, name):
    m, d = x.shape
    ff = w1.shape[2]
    row = lambda i, f: (i, 0)
    const = lambda i, f: (0, 0)
    return pl.pallas_call(
        _mlp_kernel,
        grid=(m // tm, ff // tf),
        in_specs=[pl.BlockSpec((tm, d), row),
                  pl.BlockSpec((None, d, tf), lambda i, f: (layer, 0, f)),
                  pl.BlockSpec((None, tf, d), lambda i, f: (layer, f, 0)),
                  pl.BlockSpec((tm, d), row),
                  pl.BlockSpec((1, d), const),
                  pl.BlockSpec((1, d), const)],
        out_specs=[pl.BlockSpec((tm, d), row), pl.BlockSpec((tm, d), row)],
        out_shape=[jax.ShapeDtypeStruct((m, d), F32),
                   jax.ShapeDtypeStruct((m, d), BF16)],
        scratch_shapes=[pltpu.VMEM((tm, d), F32)],
        compiler_params=_params("parallel", "arbitrary"),
        name=name,
    )(x, w1, w2, h, g, b)


def _proj_t_kernel(w_ref, x_ref, o_ref, wb_ref):
    @pl.when(pl.program_id(0) == 0)
    def _():
        wb_ref[...] = w_ref[...].astype(BF16)

    y = jnp.dot(x_ref[...].astype(BF16), wb_ref[...], preferred_element_type=F32)
    o_ref[0] = jnp.transpose(y).astype(o_ref.dtype)


def _proj_t(w, layer, col_block, n, x, *, tm, name):
    k = w.shape[1]
    m = x.shape[0]
    return pl.pallas_call(
        _proj_t_kernel,
        grid=(m // tm,),
        in_specs=[pl.BlockSpec((None, k, n), lambda i: (layer, 0, col_block),
                               pipeline_mode=pl.Buffered(1)),
                  pl.BlockSpec((tm, k), lambda i: (i, 0))],
        out_specs=pl.BlockSpec((1, n, tm), lambda i: (i, 0, 0)),
        out_shape=jax.ShapeDtypeStruct((m // tm, n, tm), BF16),
        scratch_shapes=[pltpu.VMEM((k, n), BF16)],
        compiler_params=_params("arbitrary"),
        name=name,
    )(w, x)


def _attn_kernel(q_ref, k_ref, vt_ref, lam_ref, sub_ref, o_ref,
                 s0_ref, s1_ref, p0_ref, p1_ref, acc0_ref, acc1_ref, *, lam_init, t):
    d = DIFF_HEAD_DIM
    i = pl.program_id(2)
    s_ref, p_ref, acc_ref = (s0_ref, s1_ref), (p0_ref, p1_ref), (acc0_ref, acc1_ref)

    lam = lam_ref[...]
    lam_full = (jnp.exp(jnp.sum(lam[0:1] * lam[1:2], keepdims=True))
                - jnp.exp(jnp.sum(lam[2:3] * lam[3:4], keepdims=True)) + lam_init)

    key_chunk = lax.broadcasted_iota(jnp.int32, (t, t), 0) // CHUNK
    qry_chunk = lax.broadcasted_iota(jnp.int32, (t, t), 1) // CHUNK
    allowed = key_chunk <= qry_chunk

    def scores(a, j):
        off = pl.multiple_of(j * t, t)
        s_ref[a][:, :t] = lax.dot_general(k_ref[pl.ds(off, t), a * d:(a + 1) * d],
                                          q_ref[:, a * d:(a + 1) * d], _NT,
                                          preferred_element_type=F32)

    def softmax(a, m, masked):
        st = s_ref[a][:, :t]
        if masked:
            st = jnp.where(allowed, st, -jnp.inf)
        mn = jnp.maximum(m, jnp.max(st, axis=0, keepdims=True))
        p_ref[a][:, :t] = jnp.exp2(st - mn).astype(BF16)
        return mn, jnp.exp2(m - mn)

    ones_rows = jnp.ones((SUM_ROWS, t), BF16)

    def weighted_values(a, j, c):
        vt1 = jnp.concatenate([vt_ref[j], ones_rows], axis=0)
        acc_ref[a][...] = c * acc_ref[a][...] + jnp.dot(vt1, p_ref[a][:, :t],
                                                         preferred_element_type=F32)

    def step(j, carry):
        m0, m1, c1 = carry
        scores(1, j)
        m0, c0 = softmax(0, m0, masked=False)
        weighted_values(1, jnp.maximum(j - 1, 0), c1)
        scores(0, j + 1)
        m1, c1 = softmax(1, m1, masked=False)
        weighted_values(0, j, c0)
        return m0, m1, c1

    acc0_ref[...] = jnp.zeros_like(acc0_ref)
    acc1_ref[...] = jnp.zeros_like(acc1_ref)
    p1_ref[:, :t] = jnp.zeros((t, t), BF16)
    scores(0, 0)
    neg = jnp.full((1, t), -jnp.inf, F32)
    m0, m1, c1 = lax.fori_loop(0, i, step, (neg, neg, jnp.ones((1, t), F32)))

    scores(1, i)
    m0, c0 = softmax(0, m0, masked=True)
    weighted_values(1, jnp.maximum(i - 1, 0), c1)
    m1, c1 = softmax(1, m1, masked=True)
    weighted_values(0, i, c0)
    weighted_values(1, i, c1)

    h2d = 2 * d
    ot = (acc0_ref[:h2d, :] / acc0_ref[h2d:h2d + 1, :]
          - lam_full * (acc1_ref[:h2d, :] / acc1_ref[h2d:h2d + 1, :]))
    ms = jnp.mean(ot * ot, axis=0, keepdims=True)
    o = jnp.transpose(ot * lax.rsqrt(ms + RMS_EPS))
    o_ref[...] = (o * sub_ref[...] * (1.0 - lam_init)).astype(o_ref.dtype)


def _diff_attention(qk, vt, lam, subln, *, batch, seq, lam_init, t, name):
    h2d = 2 * DIFF_HEAD_DIM
    nh = DIFF_HEADS
    nq = seq // t
    return pl.pallas_call(
        functools.partial(_attn_kernel, lam_init=lam_init, t=t),
        grid=(batch, nh, nq),
        in_specs=[pl.BlockSpec((t, h2d), lambda b, h, i: (b * nq + i, h)),
                  pl.BlockSpec((seq, h2d), lambda b, h, i: (b, nh + h)),
                  pl.BlockSpec((nq, h2d, t), lambda b, h, i: (b, h, 0)),
                  pl.BlockSpec((4, DIFF_HEAD_DIM), lambda b, h, i: (0, 0)),
                  pl.BlockSpec((1, h2d), lambda b, h, i: (0, 0))],
        out_specs=pl.BlockSpec((t, h2d), lambda b, h, i: (b * nq + i, h)),
        out_shape=jax.ShapeDtypeStruct((batch * seq, nh * h2d), BF16),
        scratch_shapes=[pltpu.VMEM((t, t + LANES), F32), pltpu.VMEM((t, t + LANES), F32),
                        pltpu.VMEM((t, t + LANES), BF16), pltpu.VMEM((t, t + LANES), BF16),
                        pltpu.VMEM((h2d + SUM_ROWS, t), F32),
                        pltpu.VMEM((h2d + SUM_ROWS, t), F32)],
        compiler_params=_params("parallel", "parallel", "arbitrary"),
        name=name,
    )(qk, qk, vt, lam, subln)


def _gla_kernel(q_ref, k_ref, v_ref, r_ref, z_ref, wg_ref, bg_ref, gn_ref, o_ref,
                st_ref, oacc_ref, *, t):
    @pl.when(pl.program_id(2) == 0)
    def _():
        st_ref[...] = jnp.zeros_like(st_ref)

    gk = jnp.dot(z_ref[...], wg_ref[...], preferred_element_type=F32) + bg_ref[...]
    log_a = jax.nn.log_sigmoid(gk) / GLA_GATE_TAU

    pos = lax.broadcasted_iota(jnp.int32, (t, GLA_DK), 0) % CHUNK
    b = log_a
    shift = 1
    while shift < CHUNK:
        b = b + jnp.where(pos >= shift, pltpu.roll(b, shift, axis=0), 0.0)
        shift *= 2

    tr = lax.broadcasted_iota(jnp.int32, (CHUNK, CHUNK), 0)
    tc = lax.broadcasted_iota(jnp.int32, (CHUNK, CHUNK), 1)
    causal = tc <= tr

    for c in range(t // CHUNK):
        sl = slice(c * CHUNK, (c + 1) * CHUNK)
        bc = b[sl]
        bl = bc[CHUNK - 1:CHUNK]
        qc = q_ref[sl, :].astype(F32)
        kc = k_ref[sl, :].astype(F32)
        vc = v_ref[sl, :]
        q_dec = (qc * jnp.exp(bc)).astype(BF16)
        k_inv = (kc * jnp.exp(-bc)).astype(BF16)
        k_st = (kc * jnp.exp(bl - bc)).astype(BF16)
        attn = lax.dot_general(q_dec, k_inv, _NT, preferred_element_type=F32)
        attn = jnp.where(causal, attn, 0.0).astype(BF16)
        st = st_ref[...]
        oacc_ref[sl, :] = (
            jnp.dot(attn, vc, preferred_element_type=F32)
            + lax.dot_general(q_dec, st.astype(BF16), _NT, preferred_element_type=F32))
        st_ref[...] = jnp.exp(bl) * st + lax.dot_general(
            vc, k_st, _TN, preferred_element_type=F32)

    o = oacc_ref[...]
    ms = jnp.mean(o * o, axis=-1, keepdims=True)
    o = o * lax.rsqrt(ms + RMS_EPS) * gn_ref[...]
    r = r_ref[...].astype(F32)
    o = o * (r * jax.nn.sigmoid(r))
    o_ref[...] = o.astype(o_ref.dtype)


def _gla(proj, z, wg, bg, gn, *, batch, seq, t, name):
    nh = GLA_HEADS
    nt = seq // t
    tok = lambda off: (lambda b, h, i: (b * nt + i, off + h))
    return pl.pallas_call(
        functools.partial(_gla_kernel, t=t),
        grid=(batch, nh, nt),
        in_specs=[pl.BlockSpec((t, GLA_DK), tok(0)),
                  pl.BlockSpec((t, GLA_DK), tok(nh)),
                  pl.BlockSpec((t, GLA_DV), tok(nh)),
                  pl.BlockSpec((t, GLA_DV), tok(2 * nh)),
                  pl.BlockSpec((t, LANES), lambda b, h, i: (b * nt + i, 0)),
                  pl.BlockSpec((LANES, GLA_DK), lambda b, h, i: (0, h)),
                  pl.BlockSpec((1, GLA_DK), lambda b, h, i: (0, h)),
                  pl.BlockSpec((1, GLA_DV), lambda b, h, i: (0, 0))],
        out_specs=pl.BlockSpec((t, GLA_DV), lambda b, h, i: (b * nt + i, h)),
        out_shape=jax.ShapeDtypeStruct((batch * seq, GLA_VAL_DIM), BF16),
        scratch_shapes=[pltpu.VMEM((GLA_DV, GLA_DK), F32),
                        pltpu.VMEM((t, GLA_DV), F32)],
        compiler_params=_params("parallel", "parallel", "arbitrary"),
        name=name,
    )(proj, proj, proj, proj, z, wg, bg, gn)


def kernel(x, attn_w_qkv, attn_w_o, attn_lambda, attn_subln, gla_w_in, gla_w_gk2, gla_b_gk,
           gla_norm, gla_w_o, mlp_w1, mlp_w2, ln_g, ln_b):
    batch, seq, d = x.shape
    m = batch * seq
    h = x.reshape(m, d)
    hb = h

    qk_cols = 4 * DIFF_HEADS * DIFF_HEAD_DIM
    q_cols = qk_cols // 2
    qkv_scale = jnp.concatenate(
        [jnp.full((1, q_cols), DIFF_HEAD_DIM ** -0.5 * math.log2(math.e), F32),
         jnp.ones((1, attn_w_qkv.shape[-1] - q_cols), F32)], axis=-1)
    main_cols = 2 * GLA_KEY_DIM + 2 * GLA_VAL_DIM
    gla_scale = jnp.concatenate(
        [jnp.full((1, GLA_KEY_DIM), GLA_DK ** -0.5, F32),
         jnp.ones((1, main_cols - GLA_KEY_DIM), F32)], axis=-1)
    ones_lanes = jnp.ones((1, LANES), F32)

    w_qkv = attn_w_qkv
    v_cols = w_qkv.shape[2] - qk_cols
    w_attn_o = attn_w_o.astype(BF16)
    w_in_t = jnp.swapaxes(gla_w_in, 1, 2)
    w_z_t = jnp.pad(w_in_t[:, main_cols:, :],
                    ((0, 0), (0, LANES - GLA_GATE_RANK), (0, 0))).astype(BF16)
    w_gk2 = jnp.pad(gla_w_gk2, ((0, 0), (0, LANES - GLA_GATE_RANK), (0, 0))).astype(BF16)
    w_gla_o = gla_w_o.astype(BF16)
    w1 = mlp_w1.astype(BF16)
    w2 = mlp_w2.astype(BF16)

    for i in range(DEPTH):
        j = i // 2
        if i % 2 == 0:
            lam_init = 0.8 - 0.6 * math.exp(-0.3 * i)
            qk = _proj(hb, w_qkv, j, qkv_scale, n=qk_cols, tm=1024, tn=1024,
                       name=f"qk_proj_{i}")
            vt = _proj_t(w_qkv, j, qk_cols // v_cols, v_cols, hb, tm=ATTN_BLOCK,
                         name=f"vt_proj_{i}")
            y = _diff_attention(qk, vt, attn_lambda[j], attn_subln[j].reshape(1, -1),
                                batch=batch, seq=seq, lam_init=lam_init, t=ATTN_BLOCK,
                                name=f"diff_attn_{i}")
            w_o = w_attn_o
        else:
            proj = _proj(hb, w_in_t, j, gla_scale, n=main_cols, tm=1024, tn=1024,
                         w_is_transposed=True, name=f"gla_proj_{i}")
            z = _proj(hb, w_z_t, j, ones_lanes, tm=1024, tn=LANES, w_is_transposed=True,
                      name=f"gla_gate_proj_{i}")
            y = _gla(proj, z, w_gk2[j], gla_b_gk[j].reshape(1, -1), gla_norm[j].reshape(1, -1),
                     batch=batch, seq=seq, t=512, name=f"gla_{i}")
            w_o = w_gla_o
        h, hb = _out_ln(y, w_o, j, h, ln_g[i, 0].reshape(1, -1), ln_b[i, 0].reshape(1, -1),
                        tm=512, sub=256, name=f"mixer_out_ln_{i}")
        h, hb = _mlp(hb, w1, w2, i, h, ln_g[i, 1].reshape(1, -1), ln_b[i, 1].reshape(1, -1),
                     tm=512, tf=1024, name=f"mlp_ln_{i}")
    return h.reshape(batch, seq, d)
```

```python
import functools
import math

import jax
import jax.numpy as jnp
from jax import lax
from jax.experimental import pallas as pl
from jax.experimental.pallas import tpu as pltpu

D_MODEL = 2048
DEPTH = 4
CHUNK = 64

DIFF_HEAD_DIM = 128
DIFF_HEADS = D_MODEL // (2 * DIFF_HEAD_DIM)

GLA_HEADS = 4
GLA_KEY_DIM = D_MODEL // 2
GLA_VAL_DIM = D_MODEL
GLA_DK = GLA_KEY_DIM // GLA_HEADS
GLA_DV = GLA_VAL_DIM // GLA_HEADS
GLA_GATE_RANK = 16
GLA_GATE_TAU = 16.0
GLA_HEADS_PER_STEP = 2

D_FF = 4 * D_MODEL

DN_ALPHA = (2.0 * DEPTH) ** 0.25
LN_EPS = 1e-5
RMS_EPS = 1e-6

ATTN_BLOCK = 512
ATTN_HEADS_PER_STEP = 2
SUM_ROWS = 16
LANES = 128
VMEM_LIMIT = 56 * 1024 * 1024

F32 = jnp.float32
BF16 = jnp.bfloat16

_NT = (((1,), (1,)), ((), ()))
_TN = (((0,), (0,)), ((), ()))


def _params(*sem):
    return pltpu.CompilerParams(dimension_semantics=sem, vmem_limit_bytes=VMEM_LIMIT)


def _layer_norm(xf, g, b):
    mu = jnp.mean(xf, axis=-1, keepdims=True)
    xc = xf - mu
    var = jnp.mean(xc * xc, axis=-1, keepdims=True)
    return xc * lax.rsqrt(var + LN_EPS) * g + b


def _proj_kernel(x_ref, w_ref, s_ref, o_ref, wb_ref, *, w_is_transposed):
    @pl.when(pl.program_id(1) == 0)
    def _():
        wb_ref[...] = w_ref[...].astype(BF16)

    dims = _NT if w_is_transposed else (((1,), (0,)), ((), ()))
    acc = lax.dot_general(x_ref[...].astype(BF16), wb_ref[...], dims,
                          preferred_element_type=F32)
    o_ref[...] = (acc * s_ref[...]).astype(o_ref.dtype)


def _proj(x, w, layer, colscale, *, tm, tn, name, n=None, w_is_transposed=False):
    m, k = x.shape
    if n is None:
        n = w.shape[1] if w_is_transposed else w.shape[2]
    if w_is_transposed:
        w_spec = pl.BlockSpec((None, tn, k), lambda j, i: (layer, j, 0))
    else:
        w_spec = pl.BlockSpec((None, k, tn), lambda j, i: (layer, 0, j))
    return pl.pallas_call(
        functools.partial(_proj_kernel, w_is_transposed=w_is_transposed),
        grid=(n // tn, m // tm),
        in_specs=[pl.BlockSpec((tm, k), lambda j, i: (i, 0)),
                  w_spec,
                  pl.BlockSpec((1, tn), lambda j, i: (0, j))],
        out_specs=pl.BlockSpec((tm, tn), lambda j, i: (i, j)),
        out_shape=jax.ShapeDtypeStruct((m, n), BF16),
        scratch_shapes=[pltpu.VMEM(w_spec.block_shape[1:], BF16)],
        compiler_params=_params("parallel", "arbitrary"),
        name=name,
    )(x, w, colscale)


def _out_ln_kernel(a_ref, w_ref, h_ref, g_ref, b_ref, of_ref, ob_ref, *, sub):
    for s in range(a_ref.shape[0] // sub):
        rows = slice(s * sub, (s + 1) * sub)
        y = jnp.dot(a_ref[rows, :], w_ref[...], preferred_element_type=F32)
        hn = _layer_norm(DN_ALPHA * h_ref[rows, :] + y, g_ref[...], b_ref[...])
        of_ref[rows, :] = hn
        ob_ref[rows, :] = hn.astype(BF16)


def _out_ln(a, w, layer, h, g, b, *, tm, sub, name):
    m, k = a.shape
    d = w.shape[2]
    row = lambda i: (i, 0)
    const = lambda i: (0, 0)
    return pl.pallas_call(
        functools.partial(_out_ln_kernel, sub=sub),
        grid=(m // tm,),
        in_specs=[pl.BlockSpec((tm, k), row),
                  pl.BlockSpec((None, k, d), lambda i: (layer, 0, 0)),
                  pl.BlockSpec((tm, d), row),
                  pl.BlockSpec((1, d), const),
                  pl.BlockSpec((1, d), const)],
        out_specs=[pl.BlockSpec((tm, d), row), pl.BlockSpec((tm, d), row)],
        out_shape=[jax.ShapeDtypeStruct((m, d), F32),
                   jax.ShapeDtypeStruct((m, d), BF16)],
        compiler_params=_params("parallel"),
        name=name,
    )(a, w, h, g, b)


def _mlp_kernel(x_ref, w1_ref, w2_ref, h_ref, g_ref, b_ref, of_ref, ob_ref, acc_ref):
    f = pl.program_id(1)

    @pl.when(f == 0)
    def _():
        acc_ref[...] = jnp.zeros_like(acc_ref)

    u = jnp.dot(x_ref[...], w1_ref[...], preferred_element_type=F32)
    u = jnp.maximum(u, 0.0)
    acc_ref[...] += jnp.dot((u * u).astype(BF16), w2_ref[...], preferred_element_type=F32)

    @pl.when(f == pl.num_programs(1) - 1)
    def _():
        hn = _layer_norm(DN_ALPHA * h_ref[...] + acc_ref[...], g_ref[...], b_ref[...])
        of_ref[...] = hn
        ob_ref[...] = hn.astype(BF16)


def _mlp(x, w1, w2, layer, h, g, b, *, tm, tf, name):
    m, d = x.shape
    ff = w1.shape[2]
    row = lambda i, f: (i, 0)
    const = lambda i, f: (0, 0)
    return pl.pallas_call(
        _mlp_kernel,
        grid=(m // tm, ff // tf),
        in_specs=[pl.BlockSpec((tm, d), row),
                  pl.BlockSpec((None, d, tf), lambda i, f: (layer, 0, f)),
                  pl.BlockSpec((None, tf, d), lambda i, f: (layer, f, 0)),
                  pl.BlockSpec((tm, d), row),
                  pl.BlockSpec((1, d), const),
                  pl.BlockSpec((1, d), const)],
        out_specs=[pl.BlockSpec((tm, d), row), pl.BlockSpec((tm, d), row)],
        out_shape=[jax.ShapeDtypeStruct((m, d), F32),
                   jax.ShapeDtypeStruct((m, d), BF16)],
        scratch_shapes=[pltpu.VMEM((tm, d), F32)],
        compiler_params=_params("parallel", "arbitrary"),
        name=name,
    )(x, w1, w2, h, g, b)


def _proj_t_kernel(w_ref, x_ref, o_ref, wb_ref):
    @pl.when(pl.program_id(0) == 0)
    def _():
        wb_ref[...] = w_ref[...].astype(BF16)

    y = jnp.dot(x_ref[...].astype(BF16), wb_ref[...], preferred_element_type=F32)
    o_ref[0] = jnp.transpose(y).astype(o_ref.dtype)


def _proj_t(w, layer, col_block, n, x, *, tm, name):
    k = w.shape[1]
    m = x.shape[0]
    return pl.pallas_call(
        _proj_t_kernel,
        grid=(m // tm,),
        in_specs=[pl.BlockSpec((None, k, n), lambda i: (layer, 0, col_block),
                               pipeline_mode=pl.Buffered(1)),
                  pl.BlockSpec((tm, k), lambda i: (i, 0))],
        out_specs=pl.BlockSpec((1, n, tm), lambda i: (i, 0, 0)),
        out_shape=jax.ShapeDtypeStruct((m // tm, n, tm), BF16),
        scratch_shapes=[pltpu.VMEM((k, n), BF16)],
        compiler_params=_params("arbitrary"),
        name=name,
    )(w, x)


def _attn_kernel(q_ref, k_ref, vt_ref, lam_ref, sub_ref, o_ref, *scratch, lam_init, t):
    d = DIFF_HEAD_DIM
    h2d = 2 * d
    i = pl.program_id(2)
    heads = range(ATTN_HEADS_PER_STEP)
    s_ref = [scratch[6 * hh:6 * hh + 2] for hh in heads]
    p_ref = [scratch[6 * hh + 2:6 * hh + 4] for hh in heads]
    acc_ref = [scratch[6 * hh + 4:6 * hh + 6] for hh in heads]

    lam = lam_ref[...]
    lam_full = (jnp.exp(jnp.sum(lam[0:1] * lam[1:2], keepdims=True))
                - jnp.exp(jnp.sum(lam[2:3] * lam[3:4], keepdims=True)) + lam_init)

    key_chunk = lax.broadcasted_iota(jnp.int32, (t, t), 0) // CHUNK
    qry_chunk = lax.broadcasted_iota(jnp.int32, (t, t), 1) // CHUNK
    allowed = key_chunk <= qry_chunk

    def cols(hh, a):
        return slice(hh * h2d + a * d, hh * h2d + (a + 1) * d)

    def scores(hh, a, j):
        off = pl.multiple_of(j * t, t)
        s_ref[hh][a][:, :t] = lax.dot_general(k_ref[pl.ds(off, t), cols(hh, a)],
                                              q_ref[:, cols(hh, a)], _NT,
                                              preferred_element_type=F32)

    def softmax(hh, a, m, masked):
        st = s_ref[hh][a][:, :t]
        if masked:
            st = jnp.where(allowed, st, -jnp.inf)
        mn = jnp.maximum(m, jnp.max(st, axis=0, keepdims=True))
        p_ref[hh][a][:, :t] = jnp.exp2(st - mn).astype(BF16)
        return mn, jnp.exp2(m - mn)

    ones_rows = jnp.ones((SUM_ROWS, t), BF16)

    def weighted_values(hh, a, j, c):
        vt1 = jnp.concatenate([vt_ref[j, hh * h2d:(hh + 1) * h2d, :], ones_rows], axis=0)
        acc = acc_ref[hh][a]
        acc[...] = c * acc[...] + jnp.dot(vt1, p_ref[hh][a][:, :t],
                                          preferred_element_type=F32)

    def block(j, state, masked, prefetch):
        m0, m1, c1 = ([s[n] for s in state] for n in range(3))
        c0 = [None] * len(heads)
        for hh in heads:
            scores(hh, 1, j)
        for hh in heads:
            m0[hh], c0[hh] = softmax(hh, 0, m0[hh], masked)
        for hh in heads:
            weighted_values(hh, 1, jnp.maximum(j - 1, 0), c1[hh])
        if prefetch:
            for hh in heads:
                scores(hh, 0, j + 1)
        for hh in heads:
            m1[hh], c1[hh] = softmax(hh, 1, m1[hh], masked)
        for hh in heads:
            weighted_values(hh, 0, j, c0[hh])
        return [(m0[hh], m1[hh], c1[hh]) for hh in heads]

    def step(j, carry):
        state = [carry[3 * hh:3 * hh + 3] for hh in heads]
        state = block(j, state, masked=False, prefetch=True)
        return tuple(x for s in state for x in s)

    neg = jnp.full((1, t), -jnp.inf, F32)
    for hh in heads:
        for a in range(2):
            acc_ref[hh][a][...] = jnp.zeros_like(acc_ref[hh][a])
        p_ref[hh][1][:, :t] = jnp.zeros((t, t), BF16)
        scores(hh, 0, 0)
    carry = lax.fori_loop(0, i, step, (neg, neg, jnp.ones((1, t), F32)) * len(heads))

    state = block(i, [carry[3 * hh:3 * hh + 3] for hh in heads], masked=True, prefetch=False)
    for hh in heads:
        weighted_values(hh, 1, i, state[hh][2])

    for hh in heads:
        acc0, acc1 = acc_ref[hh]
        ot = (acc0[:h2d, :] / acc0[h2d:h2d + 1, :]
              - lam_full * (acc1[:h2d, :] / acc1[h2d:h2d + 1, :]))
        ms = jnp.mean(ot * ot, axis=0, keepdims=True)
        o = jnp.transpose(ot * lax.rsqrt(ms + RMS_EPS))
        o_ref[:, hh * h2d:(hh + 1) * h2d] = (
            o * sub_ref[...] * (1.0 - lam_init)).astype(o_ref.dtype)


def _diff_attention(qk, vt, lam, subln, *, batch, seq, lam_init, t, name):
    h2d = 2 * DIFF_HEAD_DIM
    hps = ATTN_HEADS_PER_STEP
    ng = DIFF_HEADS // hps
    w = hps * h2d
    nq = seq // t
    head_scratch = [pltpu.VMEM((t, t + LANES), F32), pltpu.VMEM((t, t + LANES), F32),
                    pltpu.VMEM((t, t + LANES), BF16), pltpu.VMEM((t, t + LANES), BF16),
                    pltpu.VMEM((h2d + SUM_ROWS, t), F32),
                    pltpu.VMEM((h2d + SUM_ROWS, t), F32)]
    return pl.pallas_call(
        functools.partial(_attn_kernel, lam_init=lam_init, t=t),
        grid=(batch, ng, nq),
        in_specs=[pl.BlockSpec((t, w), lambda b, g, i: (b * nq + i, g)),
                  pl.BlockSpec((seq, w), lambda b, g, i: (b, ng + g)),
                  pl.BlockSpec((nq, w, t), lambda b, g, i: (b, g, 0)),
                  pl.BlockSpec((4, DIFF_HEAD_DIM), lambda b, g, i: (0, 0)),
                  pl.BlockSpec((1, h2d), lambda b, g, i: (0, 0))],
        out_specs=pl.BlockSpec((t, w), lambda b, g, i: (b * nq + i, g)),
        out_shape=jax.ShapeDtypeStruct((batch * seq, DIFF_HEADS * h2d), BF16),
        scratch_shapes=head_scratch * hps,
        compiler_params=_params("parallel", "parallel", "arbitrary"),
        name=name,
    )(qk, qk, vt, lam, subln)


def _gla_kernel(q_ref, k_ref, v_ref, r_ref, z_ref, wg_ref, bg_ref, gn_ref, o_ref,
                st_ref, oacc_ref, *, t):
    dk, dv = GLA_DK, GLA_DV

    @pl.when(pl.program_id(2) == 0)
    def _():
        st_ref[...] = jnp.zeros_like(st_ref)

    gk = jnp.dot(z_ref[...], wg_ref[...], preferred_element_type=F32) + bg_ref[...]
    log_a = jax.nn.log_sigmoid(gk) / GLA_GATE_TAU

    pos = lax.broadcasted_iota(jnp.int32, log_a.shape, 0) % CHUNK
    b = log_a
    shift = 1
    while shift < CHUNK:
        b = b + jnp.where(pos >= shift, pltpu.roll(b, shift, axis=0), 0.0)
        shift *= 2

    tr = lax.broadcasted_iota(jnp.int32, (CHUNK, CHUNK), 0)
    tc = lax.broadcasted_iota(jnp.int32, (CHUNK, CHUNK), 1)
    causal = tc <= tr

    for c in range(t // CHUNK):
        sl = slice(c * CHUNK, (c + 1) * CHUNK)
        for hh in range(GLA_HEADS_PER_STEP):
            kcols = slice(hh * dk, (hh + 1) * dk)
            vcols = slice(hh * dv, (hh + 1) * dv)
            bc = b[sl, kcols]
            bl = bc[CHUNK - 1:CHUNK]
            qc = q_ref[sl, kcols].astype(F32)
            kc = k_ref[sl, kcols].astype(F32)
            vc = v_ref[sl, vcols]
            decay = jnp.exp(bl)
            q_dec = (qc * jnp.exp(bc)).astype(BF16)
            k_inv_f = kc * jnp.exp(-bc)
            k_inv = k_inv_f.astype(BF16)
            k_st = (k_inv_f * decay).astype(BF16)
            attn = lax.dot_general(q_dec, k_inv, _NT, preferred_element_type=F32)
            attn = jnp.where(causal, attn, 0.0).astype(BF16)
            st = st_ref[hh]
            oacc_ref[sl, vcols] = (
                jnp.dot(attn, vc, preferred_element_type=F32)
                + lax.dot_general(q_dec, st.astype(BF16), _NT, preferred_element_type=F32))
            st_ref[hh] = decay * st + lax.dot_general(
                vc, k_st, _TN, preferred_element_type=F32)

    r = r_ref[...].astype(F32)
    gate = r * (0.5 * jnp.tanh(0.5 * r) + 0.5)
    for hh in range(GLA_HEADS_PER_STEP):
        vcols = slice(hh * dv, (hh + 1) * dv)
        o = oacc_ref[:, vcols]
        ms = jnp.mean(o * o, axis=-1, keepdims=True)
        o = o * lax.rsqrt(ms + RMS_EPS) * gn_ref[...]
        o_ref[:, vcols] = (o * gate[:, vcols]).astype(o_ref.dtype)


def _gla(proj, z, wg, bg, gn, *, batch, seq, t, name):
    hps = GLA_HEADS_PER_STEP
    ng = GLA_HEADS // hps
    nt = seq // t
    dk, dv = hps * GLA_DK, hps * GLA_DV
    tok = lambda off: (lambda b, g, i: (b * nt + i, off + g))
    return pl.pallas_call(
        functools.partial(_gla_kernel, t=t),
        grid=(batch, ng, nt),
        in_specs=[pl.BlockSpec((t, dk), tok(0)),
                  pl.BlockSpec((t, dk), tok(ng)),
                  pl.BlockSpec((t, dv), tok(ng)),
                  pl.BlockSpec((t, dv), tok(2 * ng)),
                  pl.BlockSpec((t, LANES), lambda b, g, i: (b * nt + i, 0)),
                  pl.BlockSpec((LANES, dk), lambda b, g, i: (0, g)),
                  pl.BlockSpec((1, dk), lambda b, g, i: (0, g)),
                  pl.BlockSpec((1, GLA_DV), lambda b, g, i: (0, 0))],
        out_specs=pl.BlockSpec((t, dv), lambda b, g, i: (b * nt + i, g)),
        out_shape=jax.ShapeDtypeStruct((batch * seq, GLA_VAL_DIM), BF16),
        scratch_shapes=[pltpu.VMEM((hps, GLA_DV, GLA_DK), F32),
                        pltpu.VMEM((t, dv), F32)],
        compiler_params=_params("parallel", "parallel", "arbitrary"),
        name=name,
    )(proj, proj, proj, proj, z, wg, bg, gn)


def kernel(x, attn_w_qkv, attn_w_o, attn_lambda, attn_subln, gla_w_in, gla_w_gk2, gla_b_gk,
           gla_norm, gla_w_o, mlp_w1, mlp_w2, ln_g, ln_b):
    batch, seq, d = x.shape
    m = batch * seq
    h = x.reshape(m, d)
    hb = h

    qk_cols = 4 * DIFF_HEADS * DIFF_HEAD_DIM
    q_cols = qk_cols // 2
    qkv_scale = jnp.concatenate(
        [jnp.full((1, q_cols), DIFF_HEAD_DIM ** -0.5 * math.log2(math.e), F32),
         jnp.ones((1, attn_w_qkv.shape[-1] - q_cols), F32)], axis=-1)
    main_cols = 2 * GLA_KEY_DIM + 2 * GLA_VAL_DIM
    gla_scale = jnp.concatenate(
        [jnp.full((1, GLA_KEY_DIM), GLA_DK ** -0.5, F32),
         jnp.ones((1, main_cols - GLA_KEY_DIM), F32)], axis=-1)
    ones_lanes = jnp.ones((1, LANES), F32)

    w_qkv = attn_w_qkv
    v_cols = w_qkv.shape[2] - qk_cols
    w_attn_o = attn_w_o.astype(BF16)
    w_in_t = jnp.swapaxes(gla_w_in, 1, 2)
    w_z_t = jnp.pad(w_in_t[:, main_cols:, :],
                    ((0, 0), (0, LANES - GLA_GATE_RANK), (0, 0))).astype(BF16)
    w_gk2 = jnp.pad(gla_w_gk2, ((0, 0), (0, LANES - GLA_GATE_RANK), (0, 0))).astype(BF16)
    w_gla_o = gla_w_o.astype(BF16)
    w1 = mlp_w1.astype(BF16)
    w2 = mlp_w2.astype(BF16)

    for i in range(DEPTH):
        j = i // 2
        if i % 2 == 0:
            lam_init = 0.8 - 0.6 * math.exp(-0.3 * i)
            qk = _proj(hb, w_qkv, j, qkv_scale, n=qk_cols, tm=1024, tn=1024,
                       name=f"qk_proj_{i}")
            vt = _proj_t(w_qkv, j, qk_cols // v_cols, v_cols, hb, tm=ATTN_BLOCK,
                         name=f"vt_proj_{i}")
            y = _diff_attention(qk, vt, attn_lambda[j], attn_subln[j].reshape(1, -1),
                                batch=batch, seq=seq, lam_init=lam_init, t=ATTN_BLOCK,
                                name=f"diff_attn_{i}")
            w_o = w_attn_o
        else:
            proj = _proj(hb, w_in_t, j, gla_scale, n=main_cols, tm=1024, tn=1024,
                         w_is_transposed=True, name=f"gla_proj_{i}")
            z = _proj(hb, w_z_t, j, ones_lanes, tm=1024, tn=LANES, w_is_transposed=True,
                      name=f"gla_gate_proj_{i}")
            y = _gla(proj, z, w_gk2[j], gla_b_gk[j].reshape(1, -1), gla_norm[j].reshape(1, -1),
                     batch=batch, seq=seq, t=512, name=f"gla_{i}")
            w_o = w_gla_o
        h, hb = _out_ln(y, w_o, j, h, ln_g[i, 0].reshape(1, -1), ln_b[i, 0].reshape(1, -1),
                        tm=512, sub=256, name=f"mixer_out_ln_{i}")
        h, hb = _mlp(hb, w1, w2, i, h, ln_g[i, 1].reshape(1, -1), ln_b[i, 1].reshape(1, -1),
                     tm=512, tf=1024, name=f"mlp_ln_{i}")
    return h.reshape(batch, seq, d)
```

```python
import functools
import math

import jax
import jax.numpy as jnp
from jax import lax
from jax.experimental import pallas as pl
from jax.experimental.pallas import tpu as pltpu

D_MODEL = 2048
DEPTH = 4
CHUNK = 64

DIFF_HEAD_DIM = 128
DIFF_HEADS = D_MODEL // (2 * DIFF_HEAD_DIM)

GLA_HEADS = 4
GLA_KEY_DIM = D_MODEL // 2
GLA_VAL_DIM = D_MODEL
GLA_DK = GLA_KEY_DIM // GLA_HEADS
GLA_DV = GLA_VAL_DIM // GLA_HEADS
GLA_GATE_RANK = 16
GLA_GATE_TAU = 16.0
GLA_HEADS_PER_STEP = 4

D_FF = 4 * D_MODEL

DN_ALPHA = (2.0 * DEPTH) ** 0.25
LN_EPS = 1e-5
RMS_EPS = 1e-6

ATTN_BLOCK = 512
ATTN_HEADS_PER_STEP = 2
SUM_ROWS = 16
LANES = 128
VMEM_LIMIT = 56 * 1024 * 1024

F32 = jnp.float32
BF16 = jnp.bfloat16

_NT = (((1,), (1,)), ((), ()))
_TN = (((0,), (0,)), ((), ()))


def _params(*sem):
    return pltpu.CompilerParams(dimension_semantics=sem, vmem_limit_bytes=VMEM_LIMIT)


def _layer_norm(xf, g, b):
    mu = jnp.mean(xf, axis=-1, keepdims=True)
    xc = xf - mu
    var = jnp.mean(xc * xc, axis=-1, keepdims=True)
    return xc * lax.rsqrt(var + LN_EPS) * g + b


def _proj_kernel(x_ref, w_ref, s_ref, o_ref, wb_ref, *, w_is_transposed):
    @pl.when(pl.program_id(1) == 0)
    def _():
        wb_ref[...] = w_ref[...].astype(BF16)

    dims = _NT if w_is_transposed else (((1,), (0,)), ((), ()))
    acc = lax.dot_general(x_ref[...].astype(BF16), wb_ref[...], dims,
                          preferred_element_type=F32)
    o_ref[...] = (acc * s_ref[...]).astype(o_ref.dtype)


def _proj(x, w, layer, colscale, *, tm, tn, name, n=None, w_is_transposed=False):
    m, k = x.shape
    if n is None:
        n = w.shape[1] if w_is_transposed else w.shape[2]
    if w_is_transposed:
        w_spec = pl.BlockSpec((None, tn, k), lambda j, i: (layer, j, 0))
    else:
        w_spec = pl.BlockSpec((None, k, tn), lambda j, i: (layer, 0, j))
    return pl.pallas_call(
        functools.partial(_proj_kernel, w_is_transposed=w_is_transposed),
        grid=(n // tn, m // tm),
        in_specs=[pl.BlockSpec((tm, k), lambda j, i: (i, 0)),
                  w_spec,
                  pl.BlockSpec((1, tn), lambda j, i: (0, j))],
        out_specs=pl.BlockSpec((tm, tn), lambda j, i: (i, j)),
        out_shape=jax.ShapeDtypeStruct((m, n), BF16),
        scratch_shapes=[pltpu.VMEM(w_spec.block_shape[1:], BF16)],
        compiler_params=_params("parallel", "arbitrary"),
        name=name,
    )(x, w, colscale)


def _out_ln_kernel(a_ref, w_ref, h_ref, g_ref, b_ref, of_ref, ob_ref, *, sub):
    for s in range(a_ref.shape[0] // sub):
        rows = slice(s * sub, (s + 1) * sub)
        y = jnp.dot(a_ref[rows, :], w_ref[...], preferred_element_type=F32)
        hn = _layer_norm(DN_ALPHA * h_ref[rows, :] + y, g_ref[...], b_ref[...])
        of_ref[rows, :] = hn
        ob_ref[rows, :] = hn.astype(BF16)


def _out_ln(a, w, layer, h, g, b, *, tm, sub, name):
    m, k = a.shape
    d = w.shape[2]
    row = lambda i: (i, 0)
    const = lambda i: (0, 0)
    return pl.pallas_call(
        functools.partial(_out_ln_kernel, sub=sub),
        grid=(m // tm,),
        in_specs=[pl.BlockSpec((tm, k), row),
                  pl.BlockSpec((None, k, d), lambda i: (layer, 0, 0)),
                  pl.BlockSpec((tm, d), row),
                  pl.BlockSpec((1, d), const),
                  pl.BlockSpec((1, d), const)],
        out_specs=[pl.BlockSpec((tm, d), row), pl.BlockSpec((tm, d), row)],
        out_shape=[jax.ShapeDtypeStruct((m, d), F32),
                   jax.ShapeDtypeStruct((m, d), BF16)],
        compiler_params=_params("parallel"),
        name=name,
    )(a, w, h, g, b)


def _mlp_kernel(x_ref, w1_ref, w2_ref, h_ref, g_ref, b_ref, of_ref, ob_ref, acc_ref):
    f = pl.program_id(1)

    @pl.when(f == 0)
    def _():
        acc_ref[...] = jnp.zeros_like(acc_ref)

    u = jnp.dot(x_ref[...], w1_ref[...], preferred_element_type=F32)
    u = jnp.maximum(u, 0.0)
    acc_ref[...] += jnp.dot((u * u).astype(BF16), w2_ref[...], preferred_element_type=F32)

    @pl.when(f == pl.num_programs(1) - 1)
    def _():
        hn = _layer_norm(DN_ALPHA * h_ref[...] + acc_ref[...], g_ref[...], b_ref[...])
        of_ref[...] = hn
        ob_ref[...] = hn.astype(BF16)


def _mlp(x, w1, w2, layer, h, g, b, *, tm, tf, name):
    m, d = x.shape
    ff = w1.shape[2]
    row = lambda i, f: (i, 0)
    const = lambda i, f: (0, 0)
    return pl.pallas_call(
        _mlp_kernel,
        grid=(m // tm, ff // tf),
        in_specs=[pl.BlockSpec((tm, d), row),
                  pl.BlockSpec((None, d, tf), lambda i, f: (layer, 0, f)),
                  pl.BlockSpec((None, tf, d), lambda i, f: (layer, f, 0)),
                  pl.BlockSpec((tm, d), row),
                  pl.BlockSpec((1, d), const),
                  pl.BlockSpec((1, d), const)],
        out_specs=[pl.BlockSpec((tm, d), row), pl.BlockSpec((tm, d), row)],
        out_shape=[jax.ShapeDtypeStruct((m, d), F32),
                   jax.ShapeDtypeStruct((m, d), BF16)],
        scratch_shapes=[pltpu.VMEM((tm, d), F32)],
        compiler_params=_params("parallel", "arbitrary"),
        name=name,
    )(x, w1, w2, h, g, b)


def _proj_t_kernel(w_ref, x_ref, o_ref, wb_ref):
    @pl.when(pl.program_id(0) == 0)
    def _():
        wb_ref[...] = w_ref[...].astype(BF16)

    y = jnp.dot(x_ref[...].astype(BF16), wb_ref[...], preferred_element_type=F32)
    o_ref[0] = jnp.transpose(y).astype(o_ref.dtype)


def _proj_t(w, layer, col_block, n, x, *, tm, name):
    k = w.shape[1]
    m = x.shape[0]
    return pl.pallas_call(
        _proj_t_kernel,
        grid=(m // tm,),
        in_specs=[pl.BlockSpec((None, k, n), lambda i: (layer, 0, col_block),
                               pipeline_mode=pl.Buffered(1)),
                  pl.BlockSpec((tm, k), lambda i: (i, 0))],
        out_specs=pl.BlockSpec((1, n, tm), lambda i: (i, 0, 0)),
        out_shape=jax.ShapeDtypeStruct((m // tm, n, tm), BF16),
        scratch_shapes=[pltpu.VMEM((k, n), BF16)],
        compiler_params=_params("arbitrary"),
        name=name,
    )(w, x)


def _attn_kernel(q_ref, k_ref, vt_ref, lam_ref, sub_ref, o_ref, *scratch, lam_init, t):
    d = DIFF_HEAD_DIM
    h2d = 2 * d
    i = pl.program_id(2)
    heads = range(ATTN_HEADS_PER_STEP)
    s_ref = [scratch[6 * hh:6 * hh + 2] for hh in heads]
    p_ref = [scratch[6 * hh + 2:6 * hh + 4] for hh in heads]
    acc_ref = [scratch[6 * hh + 4:6 * hh + 6] for hh in heads]

    lam = lam_ref[...]
    lam_full = (jnp.exp(jnp.sum(lam[0:1] * lam[1:2], keepdims=True))
                - jnp.exp(jnp.sum(lam[2:3] * lam[3:4], keepdims=True)) + lam_init)

    key_chunk = lax.broadcasted_iota(jnp.int32, (t, t), 0) // CHUNK
    qry_chunk = lax.broadcasted_iota(jnp.int32, (t, t), 1) // CHUNK
    allowed = key_chunk <= qry_chunk

    def cols(hh, a):
        return slice(hh * h2d + a * d, hh * h2d + (a + 1) * d)

    def scores(hh, a, j):
        off = pl.multiple_of(j * t, t)
        s_ref[hh][a][:, :t] = lax.dot_general(k_ref[pl.ds(off, t), cols(hh, a)],
                                              q_ref[:, cols(hh, a)], _NT,
                                              preferred_element_type=F32)

    def softmax(hh, a, m, masked):
        st = s_ref[hh][a][:, :t]
        if masked:
            st = jnp.where(allowed, st, -jnp.inf)
        mn = jnp.maximum(m, jnp.max(st, axis=0, keepdims=True))
        p_ref[hh][a][:, :t] = jnp.exp2(st - mn).astype(BF16)
        return mn, jnp.exp2(m - mn)

    ones_rows = jnp.ones((SUM_ROWS, t), BF16)

    def weighted_values(hh, a, j, c):
        vt1 = jnp.concatenate([vt_ref[j, hh * h2d:(hh + 1) * h2d, :], ones_rows], axis=0)
        acc = acc_ref[hh][a]
        acc[...] = c * acc[...] + jnp.dot(vt1, p_ref[hh][a][:, :t],
                                          preferred_element_type=F32)

    def block(j, state, masked, prefetch):
        m0, m1, c1 = ([s[n] for s in state] for n in range(3))
        c0 = [None] * len(heads)
        for hh in heads:
            scores(hh, 1, j)
        for hh in heads:
            m0[hh], c0[hh] = softmax(hh, 0, m0[hh], masked)
        for hh in heads:
            weighted_values(hh, 1, jnp.maximum(j - 1, 0), c1[hh])
        if prefetch:
            for hh in heads:
                scores(hh, 0, j + 1)
        for hh in heads:
            m1[hh], c1[hh] = softmax(hh, 1, m1[hh], masked)
        for hh in heads:
            weighted_values(hh, 0, j, c0[hh])
        return [(m0[hh], m1[hh], c1[hh]) for hh in heads]

    def step(j, carry):
        state = [carry[3 * hh:3 * hh + 3] for hh in heads]
        state = block(j, state, masked=False, prefetch=True)
        return tuple(x for s in state for x in s)

    neg = jnp.full((1, t), -jnp.inf, F32)
    for hh in heads:
        for a in range(2):
            acc_ref[hh][a][...] = jnp.zeros_like(acc_ref[hh][a])
        p_ref[hh][1][:, :t] = jnp.zeros((t, t), BF16)
        scores(hh, 0, 0)
    carry = lax.fori_loop(0, i, step, (neg, neg, jnp.ones((1, t), F32)) * len(heads))

    state = block(i, [carry[3 * hh:3 * hh + 3] for hh in heads], masked=True, prefetch=False)
    for hh in heads:
        weighted_values(hh, 1, i, state[hh][2])

    for hh in heads:
        acc0, acc1 = acc_ref[hh]
        ot = (acc0[:h2d, :] / acc0[h2d:h2d + 1, :]
              - lam_full * (acc1[:h2d, :] / acc1[h2d:h2d + 1, :]))
        ms = jnp.mean(ot * ot, axis=0, keepdims=True)
        o = jnp.transpose(ot * lax.rsqrt(ms + RMS_EPS))
        o_ref[:, hh * h2d:(hh + 1) * h2d] = (
            o * sub_ref[...] * (1.0 - lam_init)).astype(o_ref.dtype)


def _diff_attention(qk, vt, lam, subln, *, batch, seq, lam_init, t, name):
    h2d = 2 * DIFF_HEAD_DIM
    hps = ATTN_HEADS_PER_STEP
    ng = DIFF_HEADS // hps
    w = hps * h2d
    nq = seq // t
    head_scratch = [pltpu.VMEM((t, t + LANES), F32), pltpu.VMEM((t, t + LANES), F32),
                    pltpu.VMEM((t, t + LANES), BF16), pltpu.VMEM((t, t + LANES), BF16),
                    pltpu.VMEM((h2d + SUM_ROWS, t), F32),
                    pltpu.VMEM((h2d + SUM_ROWS, t), F32)]
    return pl.pallas_call(
        functools.partial(_attn_kernel, lam_init=lam_init, t=t),
        grid=(batch, ng, nq),
        in_specs=[pl.BlockSpec((t, w), lambda b, g, i: (b * nq + i, g)),
                  pl.BlockSpec((seq, w), lambda b, g, i: (b, ng + g)),
                  pl.BlockSpec((nq, w, t), lambda b, g, i: (b, g, 0)),
                  pl.BlockSpec((4, DIFF_HEAD_DIM), lambda b, g, i: (0, 0)),
                  pl.BlockSpec((1, h2d), lambda b, g, i: (0, 0))],
        out_specs=pl.BlockSpec((t, w), lambda b, g, i: (b * nq + i, g)),
        out_shape=jax.ShapeDtypeStruct((batch * seq, DIFF_HEADS * h2d), BF16),
        scratch_shapes=head_scratch * hps,
        compiler_params=_params("parallel", "parallel", "arbitrary"),
        name=name,
    )(qk, qk, vt, lam, subln)


def _gla_kernel(q_ref, k_ref, v_ref, r_ref, z_ref, wg_ref, bg_ref, gn_ref, o_ref,
                st_ref, oacc_ref, *, t):
    dk, dv = GLA_DK, GLA_DV

    @pl.when(pl.program_id(2) == 0)
    def _():
        st_ref[...] = jnp.zeros_like(st_ref)

    gk = jnp.dot(z_ref[...], wg_ref[...], preferred_element_type=F32) + bg_ref[...]
    log_a = jax.nn.log_sigmoid(gk) / GLA_GATE_TAU

    pos = lax.broadcasted_iota(jnp.int32, log_a.shape, 0) % CHUNK
    b = log_a
    shift = 1
    while shift < CHUNK:
        b = b + jnp.where(pos >= shift, pltpu.roll(b, shift, axis=0), 0.0)
        shift *= 2

    tr = lax.broadcasted_iota(jnp.int32, (CHUNK, CHUNK), 0)
    tc = lax.broadcasted_iota(jnp.int32, (CHUNK, CHUNK), 1)
    causal = tc <= tr

    for c in range(t // CHUNK):
        sl = slice(c * CHUNK, (c + 1) * CHUNK)
        for hh in range(GLA_HEADS_PER_STEP):
            kcols = slice(hh * dk, (hh + 1) * dk)
            vcols = slice(hh * dv, (hh + 1) * dv)
            bc = b[sl, kcols]
            bl = bc[CHUNK - 1:CHUNK]
            qc = q_ref[sl, kcols].astype(F32)
            kc = k_ref[sl, kcols].astype(F32)
            vc = v_ref[sl, vcols]
            decay = jnp.exp(bl)
            q_dec = (qc * jnp.exp(bc)).astype(BF16)
            k_inv_f = kc * jnp.exp(-bc)
            k_inv = k_inv_f.astype(BF16)
            k_st = (k_inv_f * decay).astype(BF16)
            attn = lax.dot_general(q_dec, k_inv, _NT, preferred_element_type=F32)
            attn = jnp.where(causal, attn, 0.0).astype(BF16)
            st = st_ref[hh]
            oacc_ref[sl, vcols] = (
                jnp.dot(attn, vc, preferred_element_type=F32)
                + lax.dot_general(q_dec, st.astype(BF16), _NT, preferred_element_type=F32))
            st_ref[hh] = decay * st + lax.dot_general(
                vc, k_st, _TN, preferred_element_type=F32)

    r = r_ref[...].astype(F32)
    gate = r * (0.5 * jnp.tanh(0.5 * r) + 0.5)
    for hh in range(GLA_HEADS_PER_STEP):
        vcols = slice(hh * dv, (hh + 1) * dv)
        o = oacc_ref[:, vcols]
        ms = jnp.mean(o * o, axis=-1, keepdims=True)
        o = o * lax.rsqrt(ms + RMS_EPS) * gn_ref[...]
        o_ref[:, vcols] = (o * gate[:, vcols]).astype(o_ref.dtype)


def _gla(proj, z, wg, bg, gn, *, batch, seq, t, name):
    hps = GLA_HEADS_PER_STEP
    ng = GLA_HEADS // hps
    nt = seq // t
    dk, dv = hps * GLA_DK, hps * GLA_DV
    tok = lambda off: (lambda b, g, i: (b * nt + i, off + g))
    return pl.pallas_call(
        functools.partial(_gla_kernel, t=t),
        grid=(batch, ng, nt),
        in_specs=[pl.BlockSpec((t, dk), tok(0)),
                  pl.BlockSpec((t, dk), tok(ng)),
                  pl.BlockSpec((t, dv), tok(ng)),
                  pl.BlockSpec((t, dv), tok(2 * ng)),
                  pl.BlockSpec((t, LANES), lambda b, g, i: (b * nt + i, 0)),
                  pl.BlockSpec((LANES, dk), lambda b, g, i: (0, g)),
                  pl.BlockSpec((1, dk), lambda b, g, i: (0, g)),
                  pl.BlockSpec((1, GLA_DV), lambda b, g, i: (0, 0))],
        out_specs=pl.BlockSpec((t, dv), lambda b, g, i: (b * nt + i, g)),
        out_shape=jax.ShapeDtypeStruct((batch * seq, GLA_VAL_DIM), BF16),
        scratch_shapes=[pltpu.VMEM((hps, GLA_DV, GLA_DK), F32),
                        pltpu.VMEM((t, dv), F32)],
        compiler_params=_params("parallel", "parallel", "arbitrary"),
        name=name,
    )(proj, proj, proj, proj, z, wg, bg, gn)


def kernel(x, attn_w_qkv, attn_w_o, attn_lambda, attn_subln, gla_w_in, gla_w_gk2, gla_b_gk,
           gla_norm, gla_w_o, mlp_w1, mlp_w2, ln_g, ln_b):
    batch, seq, d = x.shape
    m = batch * seq
    h = x.reshape(m, d)
    hb = h

    qk_cols = 4 * DIFF_HEADS * DIFF_HEAD_DIM
    q_cols = qk_cols // 2
    qkv_scale = jnp.concatenate(
        [jnp.full((1, q_cols), DIFF_HEAD_DIM ** -0.5 * math.log2(math.e), F32),
         jnp.ones((1, attn_w_qkv.shape[-1] - q_cols), F32)], axis=-1)
    main_cols = 2 * GLA_KEY_DIM + 2 * GLA_VAL_DIM
    gla_scale = jnp.concatenate(
        [jnp.full((1, GLA_KEY_DIM), GLA_DK ** -0.5, F32),
         jnp.ones((1, main_cols - GLA_KEY_DIM), F32)], axis=-1)
    ones_lanes = jnp.ones((1, LANES), F32)

    w_qkv = attn_w_qkv
    v_cols = w_qkv.shape[2] - qk_cols
    w_attn_o = attn_w_o.astype(BF16)
    w_in_t = jnp.swapaxes(gla_w_in, 1, 2)
    w_z_t = jnp.pad(w_in_t[:, main_cols:, :],
                    ((0, 0), (0, LANES - GLA_GATE_RANK), (0, 0))).astype(BF16)
    w_gk2 = jnp.pad(gla_w_gk2, ((0, 0), (0, LANES - GLA_GATE_RANK), (0, 0))).astype(BF16)
    w_gla_o = gla_w_o.astype(BF16)
    w1 = mlp_w1.astype(BF16)
    w2 = mlp_w2.astype(BF16)

    for i in range(DEPTH):
        j = i // 2
        if i % 2 == 0:
            lam_init = 0.8 - 0.6 * math.exp(-0.3 * i)
            qk = _proj(hb, w_qkv, j, qkv_scale, n=qk_cols, tm=1024, tn=1024,
                       name=f"qk_proj_{i}")
            vt = _proj_t(w_qkv, j, qk_cols // v_cols, v_cols, hb, tm=ATTN_BLOCK,
                         name=f"vt_proj_{i}")
            y = _diff_attention(qk, vt, attn_lambda[j], attn_subln[j].reshape(1, -1),
                                batch=batch, seq=seq, lam_init=lam_init, t=ATTN_BLOCK,
                                name=f"diff_attn_{i}")
            w_o = w_attn_o
        else:
            proj = _proj(hb, w_in_t, j, gla_scale, n=main_cols, tm=1024, tn=1024,
                         w_is_transposed=True, name=f"gla_proj_{i}")
            z = _proj(hb, w_z_t, j, ones_lanes, tm=1024, tn=LANES, w_is_transposed=True,
                      name=f"gla_gate_proj_{i}")
            y = _gla(proj, z, w_gk2[j], gla_b_gk[j].reshape(1, -1), gla_norm[j].reshape(1, -1),
                     batch=batch, seq=seq, t=512, name=f"gla_{i}")
            w_o = w_gla_o
        h, hb = _out_ln(y, w_o, j, h, ln_g[i, 0].reshape(1, -1), ln_b[i, 0].reshape(1, -1),
                        tm=512, sub=256, name=f"mixer_out_ln_{i}")
        h, hb = _mlp(hb, w1, w2, i, h, ln_g[i, 1].reshape(1, -1), ln_b[i, 1].reshape(1, -1),
                     tm=512, tf=1024, name=f"mlp_ln_{i}")
    return h.reshape(batch, seq, d)
```
